```python
import jax, jax.numpy as jnp
from jax import lax
import numpy as np

D_MODEL = 1024
BATCH = 8
SEQ = 8192
DEPTH = 2

HEAD_DIM = 64
N_HEADS_A = D_MODEL // (2 * HEAD_DIM)
N_HEADS_B = D_MODEL // (2 * HEAD_DIM)
N_KV_B = N_HEADS_B // 4
N_HEADS_C = D_MODEL // HEAD_DIM
WINDOW_B = 128
DILATION_PAIRS = ((128, 1), (512, 4), (2048, 16))
BLOCK = 128
D_FF = 4 * D_MODEL
EPS = 1e-6
EVEN_IN_WIDTH = 3 * N_HEADS_A * HEAD_DIM + N_HEADS_B * HEAD_DIM + 2 * N_KV_B * HEAD_DIM
EVEN_MIX_WIDTH = (N_HEADS_A + N_HEADS_B) * HEAD_DIM
ODD_IN_WIDTH = 3 * N_HEADS_C * HEAD_DIM + N_HEADS_C
ODD_MIX_WIDTH = N_HEADS_C * HEAD_DIM

kernel_name = "hybrid_dilated_swa_sink_fox_sqrelu"


def rms_norm(x, gain):
    xf = x.astype(jnp.float32)
    y = xf * lax.rsqrt(jnp.mean(xf * xf, axis=-1, keepdims=True) + EPS)
    return (y * gain.astype(jnp.float32)).astype(x.dtype)


def alibi_slopes(n):
    return jnp.asarray(2.0 ** (-8.0 * np.arange(1, n + 1) / n), dtype=jnp.float32)


def dilated_attention(q, k, v, slopes):
    bsz, seq, nh, dh = q.shape
    nblk = seq // BLOCK
    scale = dh ** -0.5
    q_blocks = jnp.moveaxis(q.reshape(bsz, nblk, BLOCK, nh, dh), 1, 0)
    slope_b = slopes[None, :, None, None]

    def block_fn(args):
        blk, qb = args
        t = blk * BLOCK + jnp.arange(BLOCK)
        parts = []
        for window, dil in DILATION_PAIRS:
            dist = jnp.arange(window // dil + 1) * dil
            idx = t[:, None] - dist[None, :]
            valid = idx >= 0
            idx = jnp.maximum(idx, 0)
            kg = k[:, idx]
            vg = v[:, idx]
            s = jnp.einsum("bqhd,bqjhd->bhqj", qb, kg).astype(jnp.float32) * scale
            s = s - slope_b * dist.astype(jnp.float32)
            s = jnp.where(valid[None, None], s, -jnp.inf)
            m = jnp.max(s, axis=-1, keepdims=True)
            p = jnp.exp(s - m)
            den = jnp.sum(p, axis=-1, keepdims=True)
            num = jnp.einsum("bhqj,bqjhd->bhqd", p.astype(v.dtype), vg).astype(jnp.float32)
            parts.append((m, num, den))
        m_all = parts[0][0]
        for m, _, _ in parts[1:]:
            m_all = jnp.maximum(m_all, m)
        num_all = 0.0
        den_all = 0.0
        for m, num, den in parts:
            w = jnp.exp(m - m_all)
            num_all = num_all + w * num
            den_all = den_all + w * den
        out = num_all / den_all
        return jnp.transpose(out, (0, 2, 1, 3)).astype(q.dtype)

    out = lax.map(block_fn, (jnp.arange(nblk), q_blocks))
    return jnp.moveaxis(out, 0, 1).reshape(bsz, seq, nh, dh)


def sliding_window_sink_attention(q, k, v, sinks, slopes):
    bsz, seq, nhq, dh = q.shape
    nkv = k.shape[2]
    grp = nhq // nkv
    nblk = seq // BLOCK
    scale = dh ** -0.5
    qb = q.reshape(bsz, nblk, BLOCK, nkv, grp, dh)
    pad = ((0, 0), (BLOCK, 0), (0, 0), (0, 0))
    kp = jnp.pad(k, pad).reshape(bsz, nblk + 1, BLOCK, nkv, dh)
    vp = jnp.pad(v, pad).reshape(bsz, nblk + 1, BLOCK, nkv, dh)
    kw = jnp.concatenate([kp[:, :-1], kp[:, 1:]], axis=2)
    vw = jnp.concatenate([vp[:, :-1], vp[:, 1:]], axis=2)
    s = jnp.einsum("bnqkgd,bnskd->bnkgqs", qb, kw).astype(jnp.float32) * scale
    qpos = jnp.arange(BLOCK)
    kpos = jnp.arange(2 * BLOCK)
    dist = qpos[:, None] - kpos[None, :] + BLOCK
    abs_k = jnp.arange(nblk)[:, None] * BLOCK + kpos[None, :] - BLOCK
    valid = ((dist >= 0) & (dist < WINDOW_B))[None] & (abs_k >= 0)[:, None, :]
    slope_kg = slopes.reshape(nkv, grp)[:, :, None, None]
    s = s - slope_kg * dist.astype(jnp.float32)
    s = jnp.where(valid[None, :, None, None], s, -jnp.inf)
    sink = sinks.astype(jnp.float32).reshape(nkv, grp)[:, :, None, None]
    m = jnp.maximum(jnp.max(s, axis=-1, keepdims=True), sink)
    p = jnp.exp(s - m)
    den = jnp.sum(p, axis=-1) + jnp.exp(sink - m)[..., 0]
    num = jnp.einsum("bnkgqs,bnskd->bnqkgd", p.astype(v.dtype), vw).astype(jnp.float32)
    out = num / jnp.transpose(den, (0, 1, 4, 2, 3))[..., None]
    return out.astype(q.dtype).reshape(bsz, seq, nhq, dh)


def forgetting_attention(q, k, v, log_f):
    bsz, seq, nh, dh = q.shape
    nblk = seq // BLOCK
    scale = dh ** -0.5
    c = jnp.cumsum(log_f, axis=1)
    c_k = jnp.transpose(c, (0, 2, 1))
    q_blocks = jnp.moveaxis(q.reshape(bsz, nblk, BLOCK, nh, dh), 1, 0)
    c_blocks = jnp.moveaxis(c.reshape(bsz, nblk, BLOCK, nh), 1, 0)
    key_pos = jnp.arange(seq)

    def block_fn(args):
        blk, qb, cq = args
        t = blk * BLOCK + jnp.arange(BLOCK)
        s = jnp.einsum("bqhd,bshd->bhqs", qb, k).astype(jnp.float32) * scale
        s = s + jnp.transpose(cq, (0, 2, 1))[..., None] - c_k[:, :, None, :]
        s = jnp.where((key_pos[None, :] <= t[:, None])[None, None], s, -jnp.inf)
        p = jax.nn.softmax(s, axis=-1)
        return jnp.einsum("bhqs,bshd->bqhd", p.astype(v.dtype), v)

    out = lax.map(block_fn, (jnp.arange(nblk), q_blocks, c_blocks))
    return jnp.moveaxis(out, 0, 1).reshape(bsz, seq, nh, dh)


def even_mixer(h, w_in, a_q_gain, a_k_gain, b_q_gain, b_k_gain, b_sinks, w_out):
    bsz, seq, _ = h.shape
    proj = jnp.einsum("bsd,de->bse", h, w_in)
    widths = [N_HEADS_A * HEAD_DIM] * 3 + [N_HEADS_B * HEAD_DIM, N_KV_B * HEAD_DIM, N_KV_B * HEAD_DIM]
    offs = np.cumsum(widths)[:-1].tolist()
    aq, ak, av, bq, bk, bv = jnp.split(proj, offs, axis=-1)
    aq = rms_norm(aq.reshape(bsz, seq, N_HEADS_A, HEAD_DIM), a_q_gain)
    ak = rms_norm(ak.reshape(bsz, seq, N_HEADS_A, HEAD_DIM), a_k_gain)
    av = av.reshape(bsz, seq, N_HEADS_A, HEAD_DIM)
    bq = rms_norm(bq.reshape(bsz, seq, N_HEADS_B, HEAD_DIM), b_q_gain)
    bk = rms_norm(bk.reshape(bsz, seq, N_KV_B, HEAD_DIM), b_k_gain)
    bv = bv.reshape(bsz, seq, N_KV_B, HEAD_DIM)
    slopes = alibi_slopes(N_HEADS_B + N_HEADS_A)
    a_out = dilated_attention(aq, ak, av, slopes[N_HEADS_B:])
    b_out = sliding_window_sink_attention(bq, bk, bv, b_sinks, slopes[:N_HEADS_B])
    y = jnp.concatenate([a_out.reshape(bsz, seq, -1), b_out.reshape(bsz, seq, -1)], axis=-1)
    return jnp.einsum("bse,ed->bsd", y, w_out)


def odd_mixer(h, w_in, b_forget, c_q_gain, c_k_gain, w_out):
    bsz, seq, _ = h.shape
    proj = jnp.einsum("bsd,de->bse", h, w_in)
    w = N_HEADS_C * HEAD_DIM
    q = rms_norm(proj[..., :w].reshape(bsz, seq, N_HEADS_C, HEAD_DIM), c_q_gain)
    k = rms_norm(proj[..., w:2 * w].reshape(bsz, seq, N_HEADS_C, HEAD_DIM), c_k_gain)
    v = proj[..., 2 * w:3 * w].reshape(bsz, seq, N_HEADS_C, HEAD_DIM)
    f_logit = proj[..., 3 * w:].astype(jnp.float32) + b_forget.astype(jnp.float32)
    log_f = jax.nn.log_sigmoid(f_logit)
    y = forgetting_attention(q, k, v, log_f)
    return jnp.einsum("bse,ed->bsd", y.reshape(bsz, seq, w), w_out)


def squared_relu_mlp(h, w_up, w_down):
    u = jnp.einsum("bsd,df->bsf", h, w_up)
    return jnp.einsum("bsf,fd->bsd", jnp.square(jax.nn.relu(u)), w_down)


def setup_inputs(seed: int = 0) -> dict:
    key = jax.random.key(seed)
    ks = jax.random.split(key, 18)
    n_even = (DEPTH + 1) // 2
    n_odd = DEPTH // 2
    d = D_MODEL

    def nrm(k, shape, fan_in):
        return jax.random.normal(k, shape, jnp.float32) * fan_in ** -0.5

    def gain(k, shape):
        return 1.0 + 0.02 * jax.random.normal(k, shape, jnp.float32)

    return {
        "x": jax.random.normal(ks[0], (BATCH, SEQ, d), jnp.float32),
        "g_mix": gain(ks[1], (DEPTH, d)),
        "g_mlp": gain(ks[2], (DEPTH, d)),
        "w_in_even": nrm(ks[3], (n_even, d, EVEN_IN_WIDTH), d),
        "a_q_gain": gain(ks[4], (n_even, HEAD_DIM)),
        "a_k_gain": gain(ks[5], (n_even, HEAD_DIM)),
        "b_q_gain": gain(ks[6], (n_even, HEAD_DIM)),
        "b_k_gain": gain(ks[7], (n_even, HEAD_DIM)),
        "b_sinks": 0.5 * jax.random.normal(ks[8], (n_even, N_HEADS_B), jnp.float32),
        "w_out_even": nrm(ks[9], (n_even, EVEN_MIX_WIDTH, d), EVEN_MIX_WIDTH),
        "w_in_odd": nrm(ks[10], (n_odd, d, ODD_IN_WIDTH), d),
        "b_forget": jax.random.uniform(ks[11], (n_odd, N_HEADS_C), jnp.float32, 2.0, 6.0),
        "c_q_gain": gain(ks[12], (n_odd, HEAD_DIM)),
        "c_k_gain": gain(ks[13], (n_odd, HEAD_DIM)),
        "w_out_odd": nrm(ks[14], (n_odd, ODD_MIX_WIDTH, d), ODD_MIX_WIDTH),
        "w_up": nrm(ks[15], (DEPTH, d, D_FF), d),
        "w_down": nrm(ks[16], (DEPTH, D_FF, d), D_FF),
    }


def reference(x, g_mix, g_mlp, w_in_even, a_q_gain, a_k_gain, b_q_gain, b_k_gain, b_sinks,
              w_out_even, w_in_odd, b_forget, c_q_gain, c_k_gain, w_out_odd, w_up, w_down):
    for layer in range(DEPTH):
        i = layer // 2
        h = rms_norm(x, g_mix[layer])
        if layer % 2 == 0:
            x = x + even_mixer(h, w_in_even[i], a_q_gain[i], a_k_gain[i], b_q_gain[i],
                               b_k_gain[i], b_sinks[i], w_out_even[i])
        else:
            x = x + odd_mixer(h, w_in_odd[i], b_forget[i], c_q_gain[i], c_k_gain[i], w_out_odd[i])
        h = rms_norm(x, g_mlp[layer])
        x = x + squared_relu_mlp(h, w_up[layer], w_down[layer])
    return x
```

```python
import functools

import numpy as np
import jax
import jax.numpy as jnp
from jax import lax
from jax.experimental import pallas as pl
from jax.experimental.pallas import tpu as pltpu

F32 = jnp.float32
BF16 = jnp.bfloat16

HEAD_DIM = 64
LANES = 128
EPS = 1e-6
NEG = -1e30
BLOCK = 128
DILATIONS = (1, 4, 16)
WINDOW_B = 128
VMEM_LIMIT = 56 * 1024 * 1024


def _alibi_slopes(n):
    return np.asarray(2.0 ** (-8.0 * np.arange(1, n + 1) / n), dtype=np.float32)


def _cparams(sem):
    return pltpu.CompilerParams(dimension_semantics=sem, vmem_limit_bytes=VMEM_LIMIT)


def _rms(x, gain):
    return x * lax.rsqrt(jnp.mean(x * x, axis=-1, keepdims=True) + EPS) * gain


def _lo_mask(shape):
    return lax.broadcasted_iota(jnp.int32, shape, len(shape) - 1) < HEAD_DIM


def _pair_norm(xc, gain2):
    lo = _lo_mask(xc.shape)
    sq = xc * xc
    s_lo = jnp.sum(jnp.where(lo, sq, 0.0), axis=-1, keepdims=True)
    s_hi = jnp.sum(jnp.where(lo, 0.0, sq), axis=-1, keepdims=True)
    r = jnp.where(lo, lax.rsqrt(s_lo / HEAD_DIM + EPS), lax.rsqrt(s_hi / HEAD_DIM + EPS))
    return xc * r * gain2


def _in_even_kernel(x_ref, g_ref, w_ref, gains_ref, o_ref):
    h = _rms(x_ref[...], g_ref[...]).astype(BF16)
    proj = jnp.dot(h, w_ref[...], preferred_element_type=F32)
    tm = proj.shape[0]
    lo = _lo_mask((tm, LANES))

    def col(c):
        return proj[:, c * LANES:(c + 1) * LANES]

    def put(c, val):
        o_ref[:, c * LANES:(c + 1) * LANES] = val.astype(o_ref.dtype)

    for c in range(4):
        put(c, _pair_norm(col(c), gains_ref[0:1, :]))
    for c in range(4, 8):
        put(c, _pair_norm(col(c), gains_ref[1:2, :]))
    for c in range(8, 12):
        put(c, col(c))
    for c in range(12, 16):
        put(c, _pair_norm(col(c), gains_ref[2:3, :]))
    bk = _pair_norm(col(16), gains_ref[3:4, :])
    bkr = pltpu.roll(bk, HEAD_DIM, 1)
    put(16, jnp.where(lo, bk, bkr))
    put(17, jnp.where(lo, bkr, bk))
    bv = col(17)
    bvr = pltpu.roll(bv, HEAD_DIM, 1)
    put(18, jnp.where(lo, bv, bvr))
    put(19, jnp.where(lo, bvr, bv))


def _in_even(x2, g, w, gains, tm):
    n, d = x2.shape
    e = w.shape[1]
    return pl.pallas_call(
        _in_even_kernel,
        grid=(n // tm,),
        in_specs=[
            pl.BlockSpec((tm, d), lambda i: (i, 0)),
            pl.BlockSpec((1, d), lambda i: (0, 0)),
            pl.BlockSpec((d, e), lambda i: (0, 0)),
            pl.BlockSpec((4, LANES), lambda i: (0, 0)),
        ],
        out_specs=pl.BlockSpec((tm, 20 * LANES), lambda i: (i, 0)),
        out_shape=jax.ShapeDtypeStruct((n, 20 * LANES), BF16),
        compiler_params=_cparams(("parallel",)),
        name="in_even",
    )(x2, g, w, gains)


def _banded_kernel(*refs, tu, dil, max_dist, slopes, kcols, vcols, use_sink):
    if use_sink:
        sink_ref, q_ref, kc_ref, kp_ref, vc_ref, vp_ref, o_ref = refs
        lse_ref = None
    else:
        q_ref, kc_ref, kp_ref, vc_ref, vp_ref, o_ref, lse_ref = refs
        sink_ref = None
    first = pl.program_id(2) == 0
    nj = tu // BLOCK
    lo = _lo_mask((BLOCK, LANES))
    qi = lax.broadcasted_iota(jnp.int32, (BLOCK, 2 * BLOCK), 0)
    kj = lax.broadcasted_iota(jnp.int32, (BLOCK, 2 * BLOCK), 1)
    dist = qi + BLOCK - kj
    valid = jnp.logical_and(dist >= 0, dist <= max_dist)
    valid_first = jnp.logical_and(valid, jnp.logical_or(kj >= BLOCK, jnp.logical_not(first)))
    distf = (dist * dil).astype(F32)

    for hp in range(len(kcols)):
        kc0, vc0 = kcols[hp], vcols[hp]
        biases = []
        for half in range(2):
            b = -slopes[2 * hp + half] * distf
            biases.append((jnp.where(valid_first, b, NEG), jnp.where(valid, b, NEG)))
        for j in range(nj):
            q2 = q_ref[j * BLOCK:(j + 1) * BLOCK, hp * LANES:(hp + 1) * LANES]
            if j == 0:
                kw = jnp.concatenate([kp_ref[:, kc0:kc0 + LANES], kc_ref[0:BLOCK, kc0:kc0 + LANES]], axis=0)
                vw = jnp.concatenate([vp_ref[:, vc0:vc0 + LANES], vc_ref[0:BLOCK, vc0:vc0 + LANES]], axis=0)
            else:
                kw = kc_ref[(j - 1) * BLOCK:(j + 1) * BLOCK, kc0:kc0 + LANES]
                vw = vc_ref[(j - 1) * BLOCK:(j + 1) * BLOCK, vc0:vc0 + LANES]
            outs, lses = [], []
            for half in range(2):
                qm = jnp.where(lo, q2, 0) if half == 0 else jnp.where(lo, 0, q2)
                s = lax.dot_general(qm, kw, (((1,), (1,)), ((), ())), preferred_element_type=F32)
                s = s + biases[half][0 if j == 0 else 1]
                m = jnp.max(s, axis=-1, keepdims=True)
                if use_sink:
                    sink = sink_ref[2 * hp + half]
                    m = jnp.maximum(m, sink)
                p = jnp.exp(s - m)
                l = jnp.sum(p, axis=-1, keepdims=True)
                if use_sink:
                    l = l + jnp.exp(sink - m)
                pv = jnp.dot(p.astype(BF16), vw, preferred_element_type=F32)
                outs.append(pv * (1.0 / l))
                lses.append(m + jnp.log(l))
            rows = slice(j * BLOCK, (j + 1) * BLOCK)
            cols = slice(hp * LANES, (hp + 1) * LANES)
            o_ref[rows, cols] = jnp.where(lo, outs[0], outs[1]).astype(o_ref.dtype)
            if lse_ref is not None:
                lse_ref[rows, cols] = jnp.where(lo, lses[0], lses[1])


def _banded(qkv, *, bsz, seq, dil, qgrp, kgrp, vgrp, kcols, vcols, max_dist, slopes, sinks=None):
    n, width = qkv.shape
    ngrp = width // 512
    u = seq // dil
    tu = min(512, u)
    view = qkv.reshape(bsz, u, dil * width)
    per = tu // BLOCK

    def spec(grp, prev):
        if prev:
            return pl.BlockSpec((None, BLOCK, 512),
                                lambda b, r, i: (b, jnp.maximum(i * per - 1, 0), r * ngrp + grp))
        return pl.BlockSpec((None, tu, 512), lambda b, r, i: (b, i, r * ngrp + grp))

    in_specs = [spec(qgrp, False), spec(kgrp, False), spec(kgrp, True), spec(vgrp, False), spec(vgrp, True)]
    args = [view] * 5
    use_sink = sinks is not None
    out_spec = pl.BlockSpec((None, tu, 512), lambda b, r, i: (b, i, r))
    if use_sink:
        in_specs = [pl.BlockSpec(memory_space=pltpu.SMEM)] + in_specs
        args = [sinks] + args
        out_specs = out_spec
        out_shape = jax.ShapeDtypeStruct((bsz, u, dil * 512), BF16)
    else:
        out_specs = [out_spec, out_spec]
        out_shape = [jax.ShapeDtypeStruct((bsz, u, dil * 512), F32)] * 2
    kern = functools.partial(_banded_kernel, tu=tu, dil=dil, max_dist=max_dist,
                             slopes=tuple(float(s) for s in slopes), kcols=kcols, vcols=vcols,
                             use_sink=use_sink)
    out = pl.pallas_call(
        kern,
        grid=(bsz, dil, u // tu),
        in_specs=in_specs,
        out_specs=out_specs,
        out_shape=out_shape,
        compiler_params=_cparams(("parallel", "parallel", "arbitrary")),
        name=f"banded_d{dil}" + ("_sink" if use_sink else ""),
    )(*args)
    if use_sink:
        return out.reshape(n, 512)
    return out[0].reshape(n, 512), out[1].reshape(n, 512)


def _post_kernel(*refs, merge, ff_chunk):
    if merge:
        (o1, l1, o4, l4, o16, l16, yb_ref, x_ref, wo_ref, g_ref, wu_ref, wd_ref, out_ref) = refs
        la, lb, lc = l1[...], l4[...], l16[...]
        mx = jnp.maximum(jnp.maximum(la, lb), lc)
        wa, wb, wc = jnp.exp(la - mx), jnp.exp(lb - mx), jnp.exp(lc - mx)
        a = (wa * o1[...] + wb * o4[...] + wc * o16[...]) / (wa + wb + wc)
        y = jnp.concatenate([a.astype(BF16), yb_ref[...]], axis=1)
    else:
        (y_ref, x_ref, wo_ref, g_ref, wu_ref, wd_ref, out_ref) = refs
        y = y_ref[...]
    x1 = x_ref[...] + jnp.dot(y, wo_ref[...], preferred_element_type=F32)
    h = _rms(x1, g_ref[...]).astype(BF16)
    acc = x1
    d_ff = wu_ref.shape[1]
    for f in range(d_ff // ff_chunk):
        cs = slice(f * ff_chunk, (f + 1) * ff_chunk)
        u = jnp.maximum(jnp.dot(h, wu_ref[:, cs], preferred_element_type=F32), 0.0)
        acc = acc + jnp.dot((u * u).astype(BF16), wd_ref[cs, :], preferred_element_type=F32)
    out_ref[...] = acc


def _post(ys, x2, w_out, g, w_up, w_down, tm, merge):
    n, d = x2.shape
    d_ff = w_up.shape[1]

    def rows(width):
        return pl.BlockSpec((tm, width), lambda i: (i, 0))

    def whole(shape):
        return pl.BlockSpec(shape, lambda i: (0, 0), pipeline_mode=pl.Buffered(1))

    in_specs = [rows(y.shape[1]) for y in ys] + [
        rows(d), whole((d, d)), whole((1, d)), whole((d, d_ff)), whole((d_ff, d))]
    return pl.pallas_call(
        functools.partial(_post_kernel, merge=merge, ff_chunk=1024),
        grid=(n // tm,),
        in_specs=in_specs,
        out_specs=rows(d),
        out_shape=jax.ShapeDtypeStruct((n, d), F32),
        compiler_params=_cparams(("parallel",)),
        name="post_merge" if merge else "post",
    )(*ys, x2, w_out, g, w_up, w_down)


def _in_odd_kernel(x_ref, g_ref, w_ref, gains_ref, bf_ref, qkv_ref, ccol_ref, crow_ref, carry_ref,
                   *, tiles_per_seq):
    i = pl.program_id(0)
    h = _rms(x_ref[...], g_ref[...]).astype(BF16)
    proj = jnp.dot(h, w_ref[...], preferred_element_type=F32)
    tm = proj.shape[0]
    d3 = proj.shape[1] - LANES
    npair = d3 // (3 * LANES)
    for c in range(3 * npair):
        xc = proj[:, c * LANES:(c + 1) * LANES]
        if c < npair:
            xc = _pair_norm(xc, gains_ref[0:1, :])
        elif c < 2 * npair:
            xc = _pair_norm(xc, gains_ref[1:2, :])
        qkv_ref[:, c * LANES:(c + 1) * LANES] = xc.astype(qkv_ref.dtype)

    z = proj[:, d3:] + bf_ref[...]
    log_f = jnp.minimum(z, 0.0) - jnp.log1p(jnp.exp(-jnp.abs(z)))
    r = lax.broadcasted_iota(jnp.int32, (tm, tm), 0)
    c = lax.broadcasted_iota(jnp.int32, (tm, tm), 1)
    tri = (c <= r).astype(F32)
    local = jnp.dot(tri, log_f, preferred_element_type=F32, precision=lax.Precision.HIGHEST)

    @pl.when(i % tiles_per_seq == 0)
    def _():
        carry_ref[...] = jnp.zeros_like(carry_ref)

    csum = local + carry_ref[0:1, :]
    carry_ref[0:1, :] = csum[tm - 1:tm, :]
    ccol_ref[...] = csum
    crow_ref[...] = csum.T[0:crow_ref.shape[0], :]


def _in_odd(x2, g, w, gains, bf, seq, tm, n_heads):
    n, d = x2.shape
    e = w.shape[1]
    tiles_per_seq = seq // tm
    bsz = n // seq
    return pl.pallas_call(
        functools.partial(_in_odd_kernel, tiles_per_seq=tiles_per_seq),
        grid=(n // tm,),
        in_specs=[
            pl.BlockSpec((tm, d), lambda i: (i, 0)),
            pl.BlockSpec((1, d), lambda i: (0, 0)),
            pl.BlockSpec((d, e), lambda i: (0, 0)),
            pl.BlockSpec((2, LANES), lambda i: (0, 0)),
            pl.BlockSpec((1, LANES), lambda i: (0, 0)),
        ],
        out_specs=[
            pl.BlockSpec((tm, 3 * d), lambda i: (i, 0)),
            pl.BlockSpec((tm, LANES), lambda i: (i, 0)),
            pl.BlockSpec((None, None, n_heads, tm), lambda i: (i // tiles_per_seq, i % tiles_per_seq, 0, 0)),
        ],
        out_shape=[
            jax.ShapeDtypeStruct((n, 3 * d), BF16),
            jax.ShapeDtypeStruct((n, LANES), F32),
            jax.ShapeDtypeStruct((bsz, tiles_per_seq, n_heads, tm), F32),
        ],
        scratch_shapes=[pltpu.VMEM((8, LANES), F32)],
        compiler_params=_cparams(("arbitrary",)),
        name="in_odd",
    )(x2, g, w, gains, bf)


def _fox_kernel(q_ref, k_ref, v_ref, cc_ref, cr_ref, o_ref, *, t):
    hp = pl.program_id(1)
    i = pl.program_id(2)
    lo = _lo_mask((t, LANES))
    q2 = q_ref[...]
    qh = (jnp.where(lo, q2, 0), jnp.where(lo, 0, q2))
    lane = lax.broadcasted_iota(jnp.int32, (t, LANES), 1)
    cc = cc_ref[...]
    cq = tuple(jnp.sum(jnp.where(lane == 2 * hp + half, cc, 0.0), axis=-1, keepdims=True)
               for half in range(2))
    row = lax.broadcasted_iota(jnp.int32, (t, t), 0)
    col = lax.broadcasted_iota(jnp.int32, (t, t), 1)
    causal = col <= row

    def step(kv, carry, masked):
        ms, ls, acc = carry
        off = pl.multiple_of(kv * t, t)
        kb = k_ref[pl.ds(off, t), :]
        vb = v_ref[pl.ds(off, t), :]
        ck = cr_ref[kv]
        new_m, new_l, alphas, pvs = [], [], [], []
        for half in range(2):
            s = lax.dot_general(qh[half], kb, (((1,), (1,)), ((), ())), preferred_element_type=F32)
            s = s + (cq[half] - ck[half:half + 1, :])
            if masked:
                s = jnp.where(causal, s, NEG)
            m_new = jnp.maximum(ms[half], jnp.max(s, axis=-1, keepdims=True))
            alpha = jnp.exp(ms[half] - m_new)
            p = jnp.exp(s - m_new)
            new_l.append(alpha * ls[half] + jnp.sum(p, axis=-1, keepdims=True))
            new_m.append(m_new)
            alphas.append(alpha)
            pvs.append(jnp.dot(p.astype(BF16), vb, preferred_element_type=F32))
        acc = jnp.where(lo, alphas[0], alphas[1]) * acc + jnp.where(lo, pvs[0], pvs[1])
        return tuple(new_m), tuple(new_l), acc

    init = ((jnp.full((t, 1), NEG, F32),) * 2, (jnp.zeros((t, 1), F32),) * 2, jnp.zeros((t, LANES), F32))
    carry = lax.fori_loop(0, i, lambda kv, c: step(kv, c, False), init)
    _, ls, acc = step(i, carry, True)
    o_ref[...] = (acc * jnp.where(lo, 1.0 / ls[0], 1.0 / ls[1])).astype(o_ref.dtype)


def _fox(qkv, ccol, crow, bsz, seq, t):
    n, width = qkv.shape
    d = width // 3
    npair = d // LANES
    nt = seq // t
    view = qkv.reshape(bsz, seq, width)
    crow5 = crow.reshape(bsz, nt, npair, 2, t)
    return pl.pallas_call(
        functools.partial(_fox_kernel, t=t),
        grid=(bsz, npair, nt),
        in_specs=[
            pl.BlockSpec((None, t, LANES), lambda b, p, i: (b, i, p)),
            pl.BlockSpec((None, seq, LANES), lambda b, p, i: (b, 0, npair + p)),
            pl.BlockSpec((None, seq, LANES), lambda b, p, i: (b, 0, 2 * npair + p)),
            pl.BlockSpec((t, LANES), lambda b, p, i: (b * nt + i, 0)),
            pl.BlockSpec((None, nt, None, 2, t), lambda b, p, i: (b, 0, p, 0, 0)),
        ],
        out_specs=pl.BlockSpec((None, t, LANES), lambda b, p, i: (b, i, p)),
        out_shape=jax.ShapeDtypeStruct((bsz, seq, d), BF16),
        compiler_params=_cparams(("parallel", "parallel", "arbitrary")),
        name="fox",
    )(view, view, view, ccol, crow5).reshape(n, d)


def _pair_gain(gain, scale=1.0):
    return jnp.tile(gain.astype(F32) * scale, 2)


def _forward(x, g_mix, g_mlp, w_in_even, a_q_gain, a_k_gain, b_q_gain, b_k_gain, b_sinks,
             w_out_even, w_in_odd, b_forget, c_q_gain, c_k_gain, w_out_odd, w_up, w_down):
    bsz, seq, d = x.shape
    n = bsz * seq
    depth = g_mix.shape[0]
    scale = HEAD_DIM ** -0.5
    n_heads_a = d // (2 * HEAD_DIM)
    n_heads_b = d // (2 * HEAD_DIM)
    n_heads_c = d // HEAD_DIM
    slopes = _alibi_slopes(n_heads_a + n_heads_b)
    x2 = x.reshape(n, d)
    tm = 512

    for layer in range(depth):
        i = layer // 2
        g1 = g_mix[layer].reshape(1, d)
        g2 = g_mlp[layer].reshape(1, d)
        wu = w_up[layer].astype(BF16)
        wd = w_down[layer].astype(BF16)
        if layer % 2 == 0:
            gains = jnp.stack([_pair_gain(a_q_gain[i], scale), _pair_gain(a_k_gain[i]),
                               _pair_gain(b_q_gain[i], scale), _pair_gain(b_k_gain[i])])
            qkv = _in_even(x2, g1, w_in_even[i].astype(BF16), gains, tm)
            ys = []
            for dil in DILATIONS:
                ys += list(_banded(qkv, bsz=bsz, seq=seq, dil=dil, qgrp=0, kgrp=1, vgrp=2,
                                   kcols=(0, 128, 256, 384), vcols=(0, 128, 256, 384),
                                   max_dist=BLOCK, slopes=slopes[n_heads_b:]))
            ys.append(_banded(qkv, bsz=bsz, seq=seq, dil=1, qgrp=3, kgrp=4, vgrp=4,
                              kcols=(0, 0, 128, 128), vcols=(256, 256, 384, 384),
                              max_dist=WINDOW_B - 1, slopes=slopes[:n_heads_b],
                              sinks=b_sinks[i].astype(F32)))
            x2 = _post(ys, x2, w_out_even[i].astype(BF16), g2, wu, wd, 256, merge=True)
        else:
            w = jnp.pad(w_in_odd[i], ((0, 0), (0, LANES - n_heads_c))).astype(BF16)
            gains = jnp.stack([_pair_gain(c_q_gain[i], scale), _pair_gain(c_k_gain[i])])
            bf = jnp.pad(b_forget[i].astype(F32), (0, LANES - n_heads_c)).reshape(1, LANES)
            qkv, ccol, crow = _in_odd(x2, g1, w, gains, bf, seq, tm, n_heads_c)
            y = _fox(qkv, ccol, crow, bsz, seq, tm)
            x2 = _post([y], x2, w_out_odd[i].astype(BF16), g2, wu, wd, 256, merge=False)
    return x2.reshape(bsz, seq, d)


def kernel(x, g_mix, g_mlp, w_in_even, a_q_gain, a_k_gain, b_q_gain, b_k_gain, b_sinks, w_out_even,
           w_in_odd, b_forget, c_q_gain, c_k_gain, w_out_odd, w_up, w_down):
    return _forward(x, g_mix, g_mlp, w_in_even, a_q_gain, a_k_gain, b_q_gain, b_k_gain, b_sinks,
                    w_out_even, w_in_odd, b_forget, c_q_gain, c_k_gain, w_out_odd, w_up, w_down)
```

```python
import functools

import numpy as np
import jax
import jax.numpy as jnp
from jax import lax
from jax.experimental import pallas as pl
from jax.experimental.pallas import tpu as pltpu

F32 = jnp.float32
BF16 = jnp.bfloat16

HEAD_DIM = 64
LANES = 128
EPS = 1e-6
NEG = -1e30
BLOCK = 128
DILATIONS = (1, 4, 16)
WINDOW_B = 128
LOG2E = float(np.log2(np.e))
VT_ROWS = 80
MAX_FIXED_SHIFT = 40.0
VMEM_LIMIT = 56 * 1024 * 1024


def _alibi_slopes(n):
    return np.asarray(2.0 ** (-8.0 * np.arange(1, n + 1) / n), dtype=np.float32)


def _cparams(sem):
    return pltpu.CompilerParams(dimension_semantics=sem, vmem_limit_bytes=VMEM_LIMIT)


def _rms(x, gain):
    return x * lax.rsqrt(jnp.mean(x * x, axis=-1, keepdims=True) + EPS) * gain


def _lo_mask(shape):
    return lax.broadcasted_iota(jnp.int32, shape, len(shape) - 1) < HEAD_DIM


def _pair_norm(xc, gain2):
    lo = _lo_mask(xc.shape)
    sq = xc * xc
    s_lo = jnp.sum(jnp.where(lo, sq, 0.0), axis=-1, keepdims=True)
    s_hi = jnp.sum(jnp.where(lo, 0.0, sq), axis=-1, keepdims=True)
    r = jnp.where(lo, lax.rsqrt(s_lo / HEAD_DIM + EPS), lax.rsqrt(s_hi / HEAD_DIM + EPS))
    return xc * r * gain2


def _in_even_kernel(x_ref, g_ref, w_ref, gains_ref, o_ref):
    h = _rms(x_ref[...], g_ref[...]).astype(BF16)
    proj = jnp.dot(h, w_ref[...], preferred_element_type=F32)
    tm = proj.shape[0]
    lo = _lo_mask((tm, LANES))

    def col(c):
        return proj[:, c * LANES:(c + 1) * LANES]

    def put(c, val):
        o_ref[:, c * LANES:(c + 1) * LANES] = val.astype(o_ref.dtype)

    for c in range(4):
        put(c, _pair_norm(col(c), gains_ref[0:1, :]))
    for c in range(4, 8):
        put(c, _pair_norm(col(c), gains_ref[1:2, :]))
    for c in range(8, 12):
        put(c, col(c))
    for c in range(12, 16):
        put(c, _pair_norm(col(c), gains_ref[2:3, :]))
    bk = _pair_norm(col(16), gains_ref[3:4, :])
    bkr = pltpu.roll(bk, HEAD_DIM, 1)
    put(16, jnp.where(lo, bk, bkr))
    put(17, jnp.where(lo, bkr, bk))
    bv = col(17)
    bvr = pltpu.roll(bv, HEAD_DIM, 1)
    put(18, jnp.where(lo, bv, bvr))
    put(19, jnp.where(lo, bvr, bv))


def _in_even(x2, g, w, gains, tm):
    n, d = x2.shape
    e = w.shape[1]
    return pl.pallas_call(
        _in_even_kernel,
        grid=(n // tm,),
        in_specs=[
            pl.BlockSpec((tm, d), lambda i: (i, 0)),
            pl.BlockSpec((1, d), lambda i: (0, 0)),
            pl.BlockSpec((d, e), lambda i: (0, 0)),
            pl.BlockSpec((4, LANES), lambda i: (0, 0)),
        ],
        out_specs=pl.BlockSpec((tm, 20 * LANES), lambda i: (i, 0)),
        out_shape=jax.ShapeDtypeStruct((n, 20 * LANES), BF16),
        compiler_params=_cparams(("parallel",)),
        name="in_even",
    )(x2, g, w, gains)


def _banded_kernel(*refs, tu, dil, max_dist, slopes, kcols, vcols, use_sink):
    if use_sink:
        sink_ref, q_ref, kc_ref, kp_ref, vc_ref, vp_ref, o_ref = refs
        lse_ref = None
    else:
        q_ref, kc_ref, kp_ref, vc_ref, vp_ref, o_ref, lse_ref = refs
        sink_ref = None
    first = pl.program_id(2) == 0
    nj = tu // BLOCK
    lo = _lo_mask((BLOCK, LANES))
    qi = lax.broadcasted_iota(jnp.int32, (BLOCK, 2 * BLOCK), 0)
    kj = lax.broadcasted_iota(jnp.int32, (BLOCK, 2 * BLOCK), 1)
    dist = qi + BLOCK - kj
    valid = jnp.logical_and(dist >= 0, dist <= max_dist)
    valid_first = jnp.logical_and(valid, jnp.logical_or(kj >= BLOCK, jnp.logical_not(first)))
    distf = (dist * dil).astype(F32)

    for hp in range(len(kcols)):
        kc0, vc0 = kcols[hp], vcols[hp]
        biases = []
        for half in range(2):
            b = -slopes[2 * hp + half] * distf
            biases.append((jnp.where(valid_first, b, NEG), jnp.where(valid, b, NEG)))
        for j in range(nj):
            q2 = q_ref[j * BLOCK:(j + 1) * BLOCK, hp * LANES:(hp + 1) * LANES]
            if j == 0:
                kw = jnp.concatenate([kp_ref[:, kc0:kc0 + LANES], kc_ref[0:BLOCK, kc0:kc0 + LANES]], axis=0)
                vw = jnp.concatenate([vp_ref[:, vc0:vc0 + LANES], vc_ref[0:BLOCK, vc0:vc0 + LANES]], axis=0)
            else:
                kw = kc_ref[(j - 1) * BLOCK:(j + 1) * BLOCK, kc0:kc0 + LANES]
                vw = vc_ref[(j - 1) * BLOCK:(j + 1) * BLOCK, vc0:vc0 + LANES]
            outs, lses = [], []
            for half in range(2):
                qm = jnp.where(lo, q2, 0) if half == 0 else jnp.where(lo, 0, q2)
                s = lax.dot_general(qm, kw, (((1,), (1,)), ((), ())), preferred_element_type=F32)
                s = s + biases[half][0 if j == 0 else 1]
                m = jnp.max(s, axis=-1, keepdims=True)
                if use_sink:
                    sink = sink_ref[2 * hp + half]
                    m = jnp.maximum(m, sink)
                p = jnp.exp(s - m)
                l = jnp.sum(p, axis=-1, keepdims=True)
                if use_sink:
                    l = l + jnp.exp(sink - m)
                pv = jnp.dot(p.astype(BF16), vw, preferred_element_type=F32)
                outs.append(pv * (1.0 / l))
                lses.append(m + jnp.log(l))
            rows = slice(j * BLOCK, (j + 1) * BLOCK)
            cols = slice(hp * LANES, (hp + 1) * LANES)
            o_ref[rows, cols] = jnp.where(lo, outs[0], outs[1]).astype(o_ref.dtype)
            if lse_ref is not None:
                lse_ref[rows, cols] = jnp.where(lo, lses[0], lses[1])


def _banded(qkv, *, bsz, seq, dil, qgrp, kgrp, vgrp, kcols, vcols, max_dist, slopes, sinks=None):
    n, width = qkv.shape
    ngrp = width // 512
    u = seq // dil
    tu = min(512, u)
    view = qkv.reshape(bsz, u, dil * width)
    per = tu // BLOCK

    def spec(grp, prev):
        if prev:
            return pl.BlockSpec((None, BLOCK, 512),
                                lambda b, r, i: (b, jnp.maximum(i * per - 1, 0), r * ngrp + grp))
        return pl.BlockSpec((None, tu, 512), lambda b, r, i: (b, i, r * ngrp + grp))

    in_specs = [spec(qgrp, False), spec(kgrp, False), spec(kgrp, True), spec(vgrp, False), spec(vgrp, True)]
    args = [view] * 5
    use_sink = sinks is not None
    out_spec = pl.BlockSpec((None, tu, 512), lambda b, r, i: (b, i, r))
    if use_sink:
        in_specs = [pl.BlockSpec(memory_space=pltpu.SMEM)] + in_specs
        args = [sinks] + args
        out_specs = out_spec
        out_shape = jax.ShapeDtypeStruct((bsz, u, dil * 512), BF16)
    else:
        out_specs = [out_spec, out_spec]
        out_shape = [jax.ShapeDtypeStruct((bsz, u, dil * 512), F32)] * 2
    kern = functools.partial(_banded_kernel, tu=tu, dil=dil, max_dist=max_dist,
                             slopes=tuple(float(s) for s in slopes), kcols=kcols, vcols=vcols,
                             use_sink=use_sink)
    out = pl.pallas_call(
        kern,
        grid=(bsz, dil, u // tu),
        in_specs=in_specs,
        out_specs=out_specs,
        out_shape=out_shape,
        compiler_params=_cparams(("parallel", "parallel", "arbitrary")),
        name=f"banded_d{dil}" + ("_sink" if use_sink else ""),
    )(*args)
    if use_sink:
        return out.reshape(n, 512)
    return out[0].reshape(n, 512), out[1].reshape(n, 512)


def _post_kernel(*refs, merge, ff_chunk):
    if merge:
        (o1, l1, o4, l4, o16, l16, yb_ref, x_ref, wo_ref, g_ref, wu_ref, wd_ref, out_ref) = refs
        la, lb, lc = l1[...], l4[...], l16[...]
        mx = jnp.maximum(jnp.maximum(la, lb), lc)
        wa, wb, wc = jnp.exp(la - mx), jnp.exp(lb - mx), jnp.exp(lc - mx)
        a = (wa * o1[...] + wb * o4[...] + wc * o16[...]) / (wa + wb + wc)
        y = jnp.concatenate([a.astype(BF16), yb_ref[...]], axis=1)
    else:
        (y_ref, x_ref, wo_ref, g_ref, wu_ref, wd_ref, out_ref) = refs
        y = y_ref[...]
    x1 = x_ref[...] + jnp.dot(y, wo_ref[...], preferred_element_type=F32)
    h = _rms(x1, g_ref[...]).astype(BF16)
    acc = x1
    d_ff = wu_ref.shape[1]
    for f in range(d_ff // ff_chunk):
        cs = slice(f * ff_chunk, (f + 1) * ff_chunk)
        u = jnp.maximum(jnp.dot(h, wu_ref[:, cs], preferred_element_type=F32), 0.0)
        acc = acc + jnp.dot((u * u).astype(BF16), wd_ref[cs, :], preferred_element_type=F32)
    out_ref[...] = acc


def _post(ys, x2, w_out, g, w_up, w_down, tm, merge):
    n, d = x2.shape
    d_ff = w_up.shape[1]

    def rows(width):
        return pl.BlockSpec((tm, width), lambda i: (i, 0))

    def whole(shape):
        return pl.BlockSpec(shape, lambda i: (0, 0), pipeline_mode=pl.Buffered(1))

    in_specs = [rows(y.shape[1]) for y in ys] + [
        rows(d), whole((d, d)), whole((1, d)), whole((d, d_ff)), whole((d_ff, d))]
    return pl.pallas_call(
        functools.partial(_post_kernel, merge=merge, ff_chunk=1024),
        grid=(n // tm,),
        in_specs=in_specs,
        out_specs=rows(d),
        out_shape=jax.ShapeDtypeStruct((n, d), F32),
        compiler_params=_cparams(("parallel",)),
        name="post_merge" if merge else "post",
    )(*ys, x2, w_out, g, w_up, w_down)


def _split3(x):
    hi = x.astype(BF16).astype(F32)
    r = x - hi
    mid = r.astype(BF16).astype(F32)
    return hi, mid, r - mid


def _in_odd_kernel(x_ref, g_ref, w_ref, gains_ref, bf_ref, qa_ref, ka_ref, vt_ref, carry_ref,
                   *, tiles_per_seq):
    i = pl.program_id(0)
    h = _rms(x_ref[...], g_ref[...]).astype(BF16)
    proj = jnp.dot(h, w_ref[...], preferred_element_type=F32)
    tm = proj.shape[0]
    d3 = proj.shape[1] - LANES
    npair = d3 // (3 * LANES)

    z = proj[:, d3:] + bf_ref[...]
    log_f = jnp.minimum(z, 0.0) - jnp.log1p(jnp.exp(-jnp.abs(z)))
    r = lax.broadcasted_iota(jnp.int32, (tm, tm), 0)
    c = lax.broadcasted_iota(jnp.int32, (tm, tm), 1)
    tri = (c <= r).astype(F32)
    local = jnp.dot(tri, log_f, preferred_element_type=F32, precision=lax.Precision.HIGHEST)

    @pl.when(i % tiles_per_seq == 0)
    def _():
        carry_ref[...] = jnp.zeros_like(carry_ref)

    csum = local + carry_ref[0:1, :]
    carry_ref[0:1, :] = csum[tm - 1:tm, :]
    c2 = csum * LOG2E

    lane = lax.broadcasted_iota(jnp.int32, (tm, LANES), 1)
    lo = lane < HEAD_DIM
    ones_rows = (lax.broadcasted_iota(jnp.int32, (VT_ROWS - HEAD_DIM, tm), 0) == 0).astype(vt_ref.dtype)
    for hp in range(npair):
        qp = _pair_norm(proj[:, hp * LANES:(hp + 1) * LANES], gains_ref[0:1, :])
        kp = _pair_norm(proj[:, (npair + hp) * LANES:(npair + hp + 1) * LANES], gains_ref[1:2, :])
        vp = proj[:, (2 * npair + hp) * LANES:(2 * npair + hp + 1) * LANES]
        qs = (qp, pltpu.roll(qp, HEAD_DIM, 1))
        ks = (kp, pltpu.roll(kp, HEAD_DIM, 1))
        vt = vp.T
        for half in range(2):
            hd = 2 * hp + half
            cb = jnp.broadcast_to(c2[:, hd:hd + 1], (tm, LANES))
            hi, mid, low = _split3(cb)
            k_aug = jnp.where(lane == 64, hi, jnp.where(lane == 65, mid, jnp.where(lane == 66, low,
                              jnp.where(lane < 71, 1.0, 0.0))))
            q_aug = jnp.where(lane < 67, -1.0, jnp.where(lane == 67, hi, jnp.where(lane == 68, mid,
                              jnp.where(lane == 69, low, jnp.where(lane == 70, gains_ref[2:3, :], 0.0)))))
            ka_ref[hd] = jnp.where(lo, ks[half], k_aug).astype(ka_ref.dtype)
            qa_ref[hd] = jnp.where(lo, qs[half], q_aug).astype(qa_ref.dtype)
            vt_ref[hd, 0:HEAD_DIM, :] = vt[half * HEAD_DIM:(half + 1) * HEAD_DIM, :].astype(vt_ref.dtype)
            vt_ref[hd, HEAD_DIM:VT_ROWS, :] = ones_rows


def _in_odd(x2, g, w, gains, bf, seq, tm, n_heads):
    n, d = x2.shape
    e = w.shape[1]
    tiles_per_seq = seq // tm
    bsz = n // seq

    def bt(i):
        return i // tiles_per_seq, i % tiles_per_seq

    return pl.pallas_call(
        functools.partial(_in_odd_kernel, tiles_per_seq=tiles_per_seq),
        grid=(n // tm,),
        in_specs=[
            pl.BlockSpec((tm, d), lambda i: (i, 0)),
            pl.BlockSpec((1, d), lambda i: (0, 0)),
            pl.BlockSpec((d, e), lambda i: (0, 0)),
            pl.BlockSpec((3, LANES), lambda i: (0, 0)),
            pl.BlockSpec((1, LANES), lambda i: (0, 0)),
        ],
        out_specs=[
            pl.BlockSpec((None, n_heads, tm, LANES), lambda i: (bt(i)[0], 0, bt(i)[1], 0)),
            pl.BlockSpec((None, n_heads, tm, LANES), lambda i: (bt(i)[0], 0, bt(i)[1], 0)),
            pl.BlockSpec((None, n_heads, None, VT_ROWS, tm), lambda i: (bt(i)[0], 0, bt(i)[1], 0, 0)),
        ],
        out_shape=[
            jax.ShapeDtypeStruct((bsz, n_heads, seq, LANES), BF16),
            jax.ShapeDtypeStruct((bsz, n_heads, seq, LANES), BF16),
            jax.ShapeDtypeStruct((bsz, n_heads, tiles_per_seq, VT_ROWS, tm), BF16),
        ],
        scratch_shapes=[pltpu.VMEM((8, LANES), F32)],
        compiler_params=_cparams(("arbitrary",)),
        name="in_odd",
    )(x2, g, w, gains, bf)


def _fox_kernel(q_ref, k_ref, vt_ref, o_ref, m_ref, acc_ref, *, tq, kt, sub, depth, online_max):
    i = pl.program_id(2)
    nsub = tq // sub
    per = tq // kt
    m_ref[...] = jnp.full(m_ref.shape, NEG, F32)
    acc_ref[...] = jnp.zeros(acc_ref.shape, F32)
    krow = lax.broadcasted_iota(jnp.int32, (sub, sub), 0)
    qcol = lax.broadcasted_iota(jnp.int32, (sub, sub), 1)
    causal = krow <= qcol

    def scores(h, j, kb):
        qs = q_ref[h, j * sub:(j + 1) * sub, :]
        return lax.dot_general(kb, qs, (((1,), (1,)), ((), ())), preferred_element_type=F32)

    def update(h, j, s, vt, masked):
        idx = h * nsub + j
        if masked:
            s = jnp.where(causal, s, NEG)
        if online_max:
            m_old = m_ref[idx]
            m_new = jnp.maximum(m_old, jnp.max(s, axis=0, keepdims=True))
            alpha = jnp.exp2(m_old - m_new)
            p = jnp.exp2(s - m_new).astype(BF16)
            m_ref[idx] = m_new
            acc_ref[idx] = alpha * acc_ref[idx] + jnp.dot(vt, p, preferred_element_type=F32)
        else:
            acc_ref[idx] += jnp.dot(vt, jnp.exp2(s).astype(BF16), preferred_element_type=F32)

    def run(work):
        pending = [scores(*w[:3]) for w in work[:depth]]
        for t, (h, j, kb, vt, masked) in enumerate(work):
            if t + depth < len(work):
                pending.append(scores(*work[t + depth][:3]))
            update(h, j, pending.pop(0), vt, masked)

    def body(kv, carry):
        off = pl.multiple_of(kv * kt, kt)
        work = []
        for h in range(2):
            kb = k_ref[h, pl.ds(off, kt), :]
            vt = vt_ref[h, kv]
            work += [(h, j, kb, vt, False) for j in range(nsub)]
        run(work)
        return carry

    lax.fori_loop(0, i * per, body, 0)

    q0 = pl.multiple_of(i * tq, tq)
    work = []
    for jj in range(nsub):
        for h in range(2):
            kb = k_ref[h, pl.ds(q0 + jj * sub, sub), :]
            lane0 = (jj * sub) % kt
            vt = vt_ref[h, i * per + (jj * sub) // kt, :, lane0:lane0 + sub]
            work += [(h, j, kb, vt, j == jj) for j in range(jj, nsub)]
    run(work)

    for j in range(nsub):
        outs = []
        for h in range(2):
            acc = acc_ref[h * nsub + j]
            outs.append(acc[0:HEAD_DIM, :] * (1.0 / acc[HEAD_DIM:HEAD_DIM + 1, :]))
        pair_t = jnp.concatenate(outs, axis=0)
        o_ref[j * sub:(j + 1) * sub, :] = pair_t.T.astype(o_ref.dtype)


def _fox(q_aug, k_aug, vt, tq, sub, online_max):
    bsz, n_heads, seq, _ = q_aug.shape
    nkt, kt = vt.shape[2], vt.shape[4]
    npair = n_heads // 2
    nsub = tq // sub
    return pl.pallas_call(
        functools.partial(_fox_kernel, tq=tq, kt=kt, sub=sub, depth=3, online_max=online_max),
        grid=(bsz, npair, seq // tq),
        in_specs=[
            pl.BlockSpec((None, 2, tq, LANES), lambda b, p, i: (b, p, i, 0)),
            pl.BlockSpec((None, 2, seq, LANES), lambda b, p, i: (b, p, 0, 0)),
            pl.BlockSpec((None, 2, nkt, VT_ROWS, kt), lambda b, p, i: (b, p, 0, 0, 0)),
        ],
        out_specs=pl.BlockSpec((None, tq, LANES), lambda b, p, i: (b, i, p)),
        out_shape=jax.ShapeDtypeStruct((bsz, seq, n_heads * HEAD_DIM), BF16),
        scratch_shapes=[pltpu.VMEM((2 * nsub, 1, sub), F32), pltpu.VMEM((2 * nsub, VT_ROWS, sub), F32)],
        compiler_params=_cparams(("parallel", "parallel", "arbitrary")),
        name="fox_online_max" if online_max else "fox",
    )(q_aug, k_aug, vt).reshape(bsz * seq, n_heads * HEAD_DIM)


def _pair_gain(gain, scale=1.0):
    return jnp.tile(gain.astype(F32) * scale, 2)


def _forward(x, g_mix, g_mlp, w_in_even, a_q_gain, a_k_gain, b_q_gain, b_k_gain, b_sinks,
             w_out_even, w_in_odd, b_forget, c_q_gain, c_k_gain, w_out_odd, w_up, w_down):
    bsz, seq, d = x.shape
    n = bsz * seq
    depth = g_mix.shape[0]
    scale = HEAD_DIM ** -0.5
    n_heads_a = d // (2 * HEAD_DIM)
    n_heads_b = d // (2 * HEAD_DIM)
    n_heads_c = d // HEAD_DIM
    slopes = _alibi_slopes(n_heads_a + n_heads_b)
    x2 = x.reshape(n, d)
    tm = 512

    for layer in range(depth):
        i = layer // 2
        g1 = g_mix[layer].reshape(1, d)
        g2 = g_mlp[layer].reshape(1, d)
        wu = w_up[layer].astype(BF16)
        wd = w_down[layer].astype(BF16)
        if layer % 2 == 0:
            gains = jnp.stack([_pair_gain(a_q_gain[i], scale), _pair_gain(a_k_gain[i]),
                               _pair_gain(b_q_gain[i], scale), _pair_gain(b_k_gain[i])])
            qkv = _in_even(x2, g1, w_in_even[i].astype(BF16), gains, tm)
            ys = []
            for dil in DILATIONS:
                ys += list(_banded(qkv, bsz=bsz, seq=seq, dil=dil, qgrp=0, kgrp=1, vgrp=2,
                                   kcols=(0, 128, 256, 384), vcols=(0, 128, 256, 384),
                                   max_dist=BLOCK, slopes=slopes[n_heads_b:]))
            ys.append(_banded(qkv, bsz=bsz, seq=seq, dil=1, qgrp=3, kgrp=4, vgrp=4,
                              kcols=(0, 0, 128, 128), vcols=(256, 256, 384, 384),
                              max_dist=WINDOW_B - 1, slopes=slopes[:n_heads_b],
                              sinks=b_sinks[i].astype(F32)))
            x2 = _post(ys, x2, w_out_even[i].astype(BF16), g2, wu, wd, 256, merge=True)
        else:
            w = jnp.pad(w_in_odd[i], ((0, 0), (0, LANES - n_heads_c))).astype(BF16)
            bound = (HEAD_DIM * scale * LOG2E) * jnp.max(jnp.abs(c_q_gain[i])) * jnp.max(jnp.abs(c_k_gain[i]))
            shift = (bound.astype(F32) * 1.02).astype(BF16).astype(F32)
            gains = jnp.stack([_pair_gain(c_q_gain[i], scale * LOG2E), _pair_gain(c_k_gain[i]),
                               jnp.full((LANES,), -1.0, F32) * shift])
            bf = jnp.pad(b_forget[i].astype(F32), (0, LANES - n_heads_c)).reshape(1, LANES)
            q_aug, k_aug, vt = _in_odd(x2, g1, w, gains, bf, seq, tm, n_heads_c)
            fox = functools.partial(_fox, tq=min(1024, seq), sub=256)
            y = lax.cond(shift <= MAX_FIXED_SHIFT,
                         lambda q, k, v: fox(q, k, v, online_max=False),
                         lambda q, k, v: fox(q, k, v, online_max=True),
                         q_aug, k_aug, vt)
            x2 = _post([y], x2, w_out_odd[i].astype(BF16), g2, wu, wd, 256, merge=False)
    return x2.reshape(bsz, seq, d)


def kernel(x, g_mix, g_mlp, w_in_even, a_q_gain, a_k_gain, b_q_gain, b_k_gain, b_sinks, w_out_even,
           w_in_odd, b_forget, c_q_gain, c_k_gain, w_out_odd, w_up, w_down):
    return _forward(x, g_mix, g_mlp, w_in_even, a_q_gain, a_k_gain, b_q_gain, b_k_gain, b_sinks,
                    w_out_even, w_in_odd, b_forget, c_q_gain, c_k_gain, w_out_odd, w_up, w_down)
```

```python
import functools

import numpy as np
import jax
import jax.numpy as jnp
from jax import lax
from jax.experimental import pallas as pl
from jax.experimental.pallas import tpu as pltpu

F32 = jnp.float32
BF16 = jnp.bfloat16

HEAD_DIM = 64
LANES = 128
EPS = 1e-6
NEG = -1e30
BLOCK = 128
DILATIONS = (1, 4, 16)
WINDOW_B = 128
LOG2E = float(np.log2(np.e))
VT_ROWS = 80
FOX_TQ = 2048
SKIP_LOG2 = 160.0
MAX_FIXED_SHIFT = 40.0
VMEM_LIMIT = 56 * 1024 * 1024


def _alibi_slopes(n):
    return np.asarray(2.0 ** (-8.0 * np.arange(1, n + 1) / n), dtype=np.float32)


def _cparams(sem):
    return pltpu.CompilerParams(dimension_semantics=sem, vmem_limit_bytes=VMEM_LIMIT)


def _rms(x, gain):
    return x * lax.rsqrt(jnp.mean(x * x, axis=-1, keepdims=True) + EPS) * gain


def _lo_mask(shape):
    return lax.broadcasted_iota(jnp.int32, shape, len(shape) - 1) < HEAD_DIM


def _pair_norm(xc, gain2):
    lo = _lo_mask(xc.shape)
    sq = xc * xc
    s_lo = jnp.sum(jnp.where(lo, sq, 0.0), axis=-1, keepdims=True)
    s_hi = jnp.sum(jnp.where(lo, 0.0, sq), axis=-1, keepdims=True)
    r = jnp.where(lo, lax.rsqrt(s_lo / HEAD_DIM + EPS), lax.rsqrt(s_hi / HEAD_DIM + EPS))
    return xc * r * gain2


def _in_even_kernel(x_ref, g_ref, w_ref, gains_ref, o_ref, o4_ref, o16_ref, s1_ref, s2_ref):
    h = _rms(x_ref[...], g_ref[...]).astype(BF16)
    proj = jnp.dot(h, w_ref[...], preferred_element_type=F32)
    tm = proj.shape[0]
    n4, n16 = tm // 4, tm // 16
    lo = _lo_mask((tm, LANES))

    def col(c):
        return proj[:, c * LANES:(c + 1) * LANES]

    def put(c, val):
        o_ref[:, c * LANES:(c + 1) * LANES] = val.astype(o_ref.dtype)
        if c >= 12:
            return
        cols = slice(c * LANES, (c + 1) * LANES)
        s1_ref[c] = val
        for r in range(4):
            part = s1_ref[c, pl.ds(r, n4, stride=4), :]
            o4_ref[r, :, cols] = part.astype(o4_ref.dtype)
            s2_ref[c, r * n4:(r + 1) * n4, :] = part
        for r in range(4):
            for r2 in range(4):
                part = s2_ref[c, pl.ds(r * n4 + r2, n16, stride=4), :]
                o16_ref[4 * r + r2, :, cols] = part.astype(o16_ref.dtype)

    for c in range(4):
        put(c, _pair_norm(col(c), gains_ref[0:1, :]))
    for c in range(4, 8):
        put(c, _pair_norm(col(c), gains_ref[1:2, :]))
    for c in range(8, 12):
        put(c, col(c))
    for c in range(12, 16):
        put(c, _pair_norm(col(c), gains_ref[2:3, :]))
    bk = _pair_norm(col(16), gains_ref[3:4, :])
    bkr = pltpu.roll(bk, HEAD_DIM, 1)
    put(16, jnp.where(lo, bk, bkr))
    put(17, jnp.where(lo, bkr, bk))
    bv = col(17)
    bvr = pltpu.roll(bv, HEAD_DIM, 1)
    put(18, jnp.where(lo, bv, bvr))
    put(19, jnp.where(lo, bvr, bv))


def _in_even(x2, g, w, gains, seq, tm):
    n, d = x2.shape
    e = w.shape[1]
    bsz = n // seq
    tiles_per_seq = seq // tm
    a_width = 12 * LANES

    def regrouped(dil):
        return pl.BlockSpec((None, dil, tm // dil, a_width),
                            lambda i: (i // tiles_per_seq, 0, i % tiles_per_seq, 0))

    return pl.pallas_call(
        _in_even_kernel,
        grid=(n // tm,),
        in_specs=[
            pl.BlockSpec((tm, d), lambda i: (i, 0)),
            pl.BlockSpec((1, d), lambda i: (0, 0)),
            pl.BlockSpec((d, e), lambda i: (0, 0)),
            pl.BlockSpec((4, LANES), lambda i: (0, 0)),
        ],
        out_specs=[pl.BlockSpec((tm, 20 * LANES), lambda i: (i, 0)), regrouped(4), regrouped(16)],
        out_shape=[jax.ShapeDtypeStruct((n, 20 * LANES), BF16),
                   jax.ShapeDtypeStruct((bsz, 4, seq // 4, a_width), BF16),
                   jax.ShapeDtypeStruct((bsz, 16, seq // 16, a_width), BF16)],
        scratch_shapes=[pltpu.VMEM((12, tm, LANES), F32), pltpu.VMEM((12, tm, LANES), F32)],
        compiler_params=_cparams(("parallel",)),
        name="in_even",
    )(x2, g, w, gains)


def _banded_kernel(*refs, tu, dil, max_dist, slopes, kcols, vcols, use_sink, merge_in, emit_lse):
    refs = list(refs)
    sink_ref = refs.pop(0) if use_sink else None
    q_ref, kc_ref, kp_ref, vc_ref, vp_ref = refs[:5]
    refs = refs[5:]
    if merge_in:
        po_ref, pl_ref = refs[:2]
        refs = refs[2:]
    o_ref = refs.pop(0)
    lse_ref = refs.pop(0) if emit_lse else None
    if merge_in:
        mo_ref, ml_ref = refs
        n4 = tu // 4
        for slab in range(4):
            cs = slice(slab * LANES, (slab + 1) * LANES)
            for r2 in range(4):
                mo_ref[slab, pl.ds(r2, n4, stride=4), :] = po_ref[r2, :, cs]
                ml_ref[slab, pl.ds(r2, n4, stride=4), :] = pl_ref[r2, :, cs]
    first = pl.program_id(2) == 0
    nj = tu // BLOCK
    lo = _lo_mask((BLOCK, LANES))
    qi = lax.broadcasted_iota(jnp.int32, (BLOCK, 2 * BLOCK), 0)
    kj = lax.broadcasted_iota(jnp.int32, (BLOCK, 2 * BLOCK), 1)
    dist = qi + BLOCK - kj
    valid = jnp.logical_and(dist >= 0, dist <= max_dist)
    valid_first = jnp.logical_and(valid, jnp.logical_or(kj >= BLOCK, jnp.logical_not(first)))
    distf = (dist * dil).astype(F32)

    for hp in range(len(kcols)):
        kc0, vc0 = kcols[hp], vcols[hp]
        biases = []
        for half in range(2):
            b = -slopes[2 * hp + half] * distf
            biases.append((jnp.where(valid_first, b, NEG), jnp.where(valid, b, NEG)))
        for j in range(nj):
            q2 = q_ref[j * BLOCK:(j + 1) * BLOCK, hp * LANES:(hp + 1) * LANES]
            if j == 0:
                kw = jnp.concatenate([kp_ref[:, kc0:kc0 + LANES], kc_ref[0:BLOCK, kc0:kc0 + LANES]], axis=0)
                vw = jnp.concatenate([vp_ref[:, vc0:vc0 + LANES], vc_ref[0:BLOCK, vc0:vc0 + LANES]], axis=0)
            else:
                kw = kc_ref[(j - 1) * BLOCK:(j + 1) * BLOCK, kc0:kc0 + LANES]
                vw = vc_ref[(j - 1) * BLOCK:(j + 1) * BLOCK, vc0:vc0 + LANES]
            outs, lses = [], []
            for half in range(2):
                qm = jnp.where(lo, q2, 0) if half == 0 else jnp.where(lo, 0, q2)
                s = lax.dot_general(qm, kw, (((1,), (1,)), ((), ())), preferred_element_type=F32)
                s = s + biases[half][0 if j == 0 else 1]
                m = jnp.max(s, axis=-1, keepdims=True)
                if use_sink:
                    sink = sink_ref[2 * hp + half]
                    m = jnp.maximum(m, sink)
                p = jnp.exp(s - m)
                l = jnp.sum(p, axis=-1, keepdims=True)
                if use_sink:
                    l = l + jnp.exp(sink - m)
                pv = jnp.dot(p.astype(BF16), vw, preferred_element_type=F32)
                outs.append(pv * (1.0 / l))
                lses.append(m + jnp.log(l))
            rows = slice(j * BLOCK, (j + 1) * BLOCK)
            cols = slice(hp * LANES, (hp + 1) * LANES)
            o_blk = jnp.where(lo, outs[0], outs[1])
            l_blk = jnp.where(lo, lses[0], lses[1])
            if merge_in:
                o_prev, l_prev = mo_ref[hp, rows, :], ml_ref[hp, rows, :]
                mx = jnp.maximum(l_blk, l_prev)
                w_own, w_prev = jnp.exp(l_blk - mx), jnp.exp(l_prev - mx)
                den = w_own + w_prev
                o_blk = (w_own * o_blk + w_prev * o_prev) * (1.0 / den)
                l_blk = mx + jnp.log(den)
            o_ref[rows, cols] = o_blk.astype(o_ref.dtype)
            if emit_lse:
                lse_ref[rows, cols] = l_blk


def _banded(src, *, dil, groups, kcols, vcols, max_dist, slopes, sinks=None, prev=None, emit_lse=False):
    bsz, _, u, width = src.shape
    tu = min(512, u)
    per = tu // BLOCK

    def spec(grp, halo):
        if halo:
            return pl.BlockSpec((None, None, BLOCK, 512),
                                lambda b, r, i: (b, r, jnp.maximum(i * per - 1, 0), grp))
        return pl.BlockSpec((None, None, tu, 512), lambda b, r, i: (b, r, i, grp))

    qg, kg, vg = groups
    in_specs = [spec(qg, False), spec(kg, False), spec(kg, True), spec(vg, False), spec(vg, True)]
    args = [src] * 5
    use_sink = sinks is not None
    if use_sink:
        in_specs = [pl.BlockSpec(memory_space=pltpu.SMEM)] + in_specs
        args = [sinks] + args
    scratch = []
    if prev is not None:
        in_specs += [pl.BlockSpec((None, 4, tu // 4, 512), lambda b, r, i: (b, r, i, 0))] * 2
        args += list(prev)
        scratch = [pltpu.VMEM((4, tu, LANES), F32)] * 2
    out_spec = pl.BlockSpec((None, None, tu, 512), lambda b, r, i: (b, r, i, 0))
    if emit_lse:
        out_specs = [out_spec, out_spec]
        out_shape = [jax.ShapeDtypeStruct((bsz, dil, u, 512), F32)] * 2
    else:
        out_specs = out_spec
        out_shape = jax.ShapeDtypeStruct((bsz, dil, u, 512), BF16)
    kern = functools.partial(_banded_kernel, tu=tu, dil=dil, max_dist=max_dist,
                             slopes=tuple(float(s) for s in slopes), kcols=kcols, vcols=vcols,
                             use_sink=use_sink, merge_in=prev is not None, emit_lse=emit_lse)
    return pl.pallas_call(
        kern,
        grid=(bsz, dil, u // tu),
        in_specs=in_specs,
        out_specs=out_specs,
        out_shape=out_shape,
        scratch_shapes=scratch,
        compiler_params=_cparams(("parallel", "parallel", "arbitrary")),
        name=f"banded_d{dil}" + ("_sink" if use_sink else ""),
    )(*args)


def _post_kernel(*refs, n_y, ff_chunk):
    y_refs = refs[:n_y]
    x_ref, wo_ref, g_ref, wu_ref, wd_ref, out_ref = refs[n_y:]
    y = y_refs[0][...] if n_y == 1 else jnp.concatenate([r[...] for r in y_refs], axis=1)
    x1 = x_ref[...] + jnp.dot(y, wo_ref[...], preferred_element_type=F32)
    h = _rms(x1, g_ref[...]).astype(BF16)
    acc = x1
    d_ff = wu_ref.shape[1]
    for f in range(d_ff // ff_chunk):
        cs = slice(f * ff_chunk, (f + 1) * ff_chunk)
        u = jnp.maximum(jnp.dot(h, wu_ref[:, cs], preferred_element_type=F32), 0.0)
        acc = acc + jnp.dot((u * u).astype(BF16), wd_ref[cs, :], preferred_element_type=F32)
    out_ref[...] = acc


def _post(ys, x2, w_out, g, w_up, w_down, tm):
    n, d = x2.shape
    d_ff = w_up.shape[1]

    def rows(width):
        return pl.BlockSpec((tm, width), lambda i: (i, 0))

    def whole(shape):
        return pl.BlockSpec(shape, lambda i: (0, 0), pipeline_mode=pl.Buffered(1))

    in_specs = [rows(y.shape[1]) for y in ys] + [
        rows(d), whole((d, d)), whole((1, d)), whole((d, d_ff)), whole((d_ff, d))]
    return pl.pallas_call(
        functools.partial(_post_kernel, n_y=len(ys), ff_chunk=1024),
        grid=(n // tm,),
        in_specs=in_specs,
        out_specs=rows(d),
        out_shape=jax.ShapeDtypeStruct((n, d), F32),
        compiler_params=_cparams(("parallel",)),
        name=f"post_{len(ys)}",
    )(*ys, x2, w_out, g, w_up, w_down)


def _split3(x):
    hi = x.astype(BF16).astype(F32)
    r = x - hi
    mid = r.astype(BF16).astype(F32)
    return hi, mid, r - mid


def _in_odd_kernel(x_ref, g_ref, w_ref, wvt_ref, eq_ref, ek_ref, gains_ref, bf_ref,
                   qa_ref, ka_ref, vt_ref, cfirst_ref, clast_ref, carry_ref, *, tiles_per_seq):
    i = pl.program_id(0)
    h = _rms(x_ref[...], g_ref[...]).astype(BF16)
    proj = jnp.dot(h, w_ref[...], preferred_element_type=F32)
    vt_all = lax.dot_general(wvt_ref[...], h, (((1,), (1,)), ((), ())),
                             preferred_element_type=F32)
    tm = proj.shape[0]
    d2 = proj.shape[1] - LANES
    npair = d2 // (2 * LANES)

    z = proj[:, d2:] + bf_ref[...]
    log_f = jnp.minimum(z, 0.0) - jnp.log1p(jnp.exp(-jnp.abs(z)))
    r = lax.broadcasted_iota(jnp.int32, (tm, tm), 0)
    c = lax.broadcasted_iota(jnp.int32, (tm, tm), 1)
    tri = (c <= r).astype(F32)
    local = jnp.dot(tri, log_f, preferred_element_type=F32, precision=lax.Precision.HIGHEST)

    @pl.when(i % tiles_per_seq == 0)
    def _():
        carry_ref[...] = jnp.zeros_like(carry_ref)

    csum = local + carry_ref[0:1, :]
    carry_ref[0:1, :] = csum[tm - 1:tm, :]
    c2 = csum * LOG2E
    cfirst_ref[...] = c2[0:1, :]
    clast_ref[...] = c2[tm - 1:tm, :]

    lane = lax.broadcasted_iota(jnp.int32, (tm, LANES), 1)
    hi, mid, low = _split3(c2)
    a = jnp.where(lane < 16, hi, jnp.where(lane < 32, pltpu.roll(mid, 16, 1),
                  jnp.where(lane < 48, pltpu.roll(low, 32, 1), jnp.where(lane == 48, 1.0, 0.0))))
    a = a.astype(BF16)
    q_spare = jnp.dot(a, eq_ref[...], preferred_element_type=F32)
    k_spare = jnp.dot(a, ek_ref[...], preferred_element_type=F32)

    lo = lane < HEAD_DIM
    ones_rows = (lax.broadcasted_iota(jnp.int32, (VT_ROWS - HEAD_DIM, tm), 0) == 0).astype(vt_ref.dtype)
    for hp in range(npair):
        qp = _pair_norm(proj[:, hp * LANES:(hp + 1) * LANES], gains_ref[0:1, :])
        kp = _pair_norm(proj[:, (npair + hp) * LANES:(npair + hp + 1) * LANES], gains_ref[1:2, :])
        qs = (qp, pltpu.roll(qp, HEAD_DIM, 1))
        ks = (kp, pltpu.roll(kp, HEAD_DIM, 1))
        for half in range(2):
            hd = 2 * hp + half
            cols = slice(hd * LANES, (hd + 1) * LANES)
            ka_ref[hd] = jnp.where(lo, ks[half], k_spare[:, cols]).astype(ka_ref.dtype)
            qa_ref[hd] = jnp.where(lo, qs[half], q_spare[:, cols]).astype(qa_ref.dtype)
            vt_ref[hd, 0:HEAD_DIM, :] = vt_all[hd * HEAD_DIM:(hd + 1) * HEAD_DIM, :].astype(vt_ref.dtype)
            vt_ref[hd, HEAD_DIM:VT_ROWS, :] = ones_rows


def _spare_lane_placement(n_heads, shift):
    eq = np.zeros((LANES, n_heads * LANES), np.float32)
    ek = np.zeros((LANES, n_heads * LANES), np.float32)
    sh = np.zeros((LANES, n_heads * LANES), np.float32)
    for h in range(n_heads):
        base = h * LANES + HEAD_DIM
        for part in range(3):
            ek[16 * part + h, base + part] = 1.0
            eq[16 * part + h, base + 3 + part] = 1.0
        eq[48, base:base + 3] = -1.0
        ek[48, base + 3:base + 7] = 1.0
        sh[48, base + 6] = -1.0
    return (jnp.asarray(eq) + jnp.asarray(sh) * shift).astype(BF16), jnp.asarray(ek).astype(BF16)


def _in_odd(x2, g, w_qkf, w_vt, eq, ek, gains, bf, seq, tm, n_heads):
    n, d = x2.shape
    tiles_per_seq = seq // tm
    bsz = n // seq

    def bt(i):
        return i // tiles_per_seq, i % tiles_per_seq

    def whole(a):
        return pl.BlockSpec(a.shape, lambda i: (0, 0))

    return pl.pallas_call(
        functools.partial(_in_odd_kernel, tiles_per_seq=tiles_per_seq),
        grid=(n // tm,),
        in_specs=[pl.BlockSpec((tm, d), lambda i: (i, 0)), whole(g), whole(w_qkf), whole(w_vt),
                  whole(eq), whole(ek), whole(gains), whole(bf)],
        out_specs=[
            pl.BlockSpec((None, n_heads, tm, LANES), lambda i: (bt(i)[0], 0, bt(i)[1], 0)),
            pl.BlockSpec((None, n_heads, tm, LANES), lambda i: (bt(i)[0], 0, bt(i)[1], 0)),
            pl.BlockSpec((None, n_heads, None, VT_ROWS, tm), lambda i: (bt(i)[0], 0, bt(i)[1], 0, 0)),
            pl.BlockSpec((None, 1, LANES), lambda i: (i, 0, 0)),
            pl.BlockSpec((None, 1, LANES), lambda i: (i, 0, 0)),
        ],
        out_shape=[
            jax.ShapeDtypeStruct((bsz, n_heads, seq, LANES), BF16),
            jax.ShapeDtypeStruct((bsz, n_heads, seq, LANES), BF16),
            jax.ShapeDtypeStruct((bsz, n_heads, tiles_per_seq, VT_ROWS, tm), BF16),
            jax.ShapeDtypeStruct((n // tm, 1, LANES), F32),
            jax.ShapeDtypeStruct((n // tm, 1, LANES), F32),
        ],
        scratch_shapes=[pltpu.VMEM((8, LANES), F32)],
        compiler_params=_cparams(("arbitrary",)),
        name="in_odd",
    )(x2, g, w_qkf, w_vt, eq, ek, gains, bf)


def _fox_kernel(cfirst_ref, clast_ref, q_ref, k_ref, vt_ref, o_ref, m_ref, acc_ref,
                *, tq, kt, sub, depth, online_max):
    assert kt % sub == 0 and tq % kt == 0
    b = pl.program_id(0)
    pair = pl.program_id(1)
    i = pl.program_id(2)
    nsub = tq // sub
    per = tq // kt
    m_ref[...] = jnp.full(m_ref.shape, NEG, F32)
    acc_ref[...] = jnp.zeros(acc_ref.shape, F32)
    krow = lax.broadcasted_iota(jnp.int32, (kt, sub), 0)
    qcol = lax.broadcasted_iota(jnp.int32, (kt, sub), 1)

    def scores(h, j, kb):
        qs = q_ref[h, j * sub:(j + 1) * sub, :]
        return lax.dot_general(kb, qs, (((1,), (1,)), ((), ())), preferred_element_type=F32)

    def update(h, j, s, vt, diag):
        idx = h * nsub + j
        if diag is not None:
            s = jnp.where(krow + diag <= qcol, s, NEG)
        if online_max:
            m_old = m_ref[idx]
            m_new = jnp.maximum(m_old, jnp.max(s, axis=0, keepdims=True))
            alpha = jnp.exp2(m_old - m_new)
            p = jnp.exp2(s - m_new).astype(BF16)
            m_ref[idx] = m_new
            acc_ref[idx] = alpha * acc_ref[idx] + jnp.dot(vt, p, preferred_element_type=F32)
        else:
            acc_ref[idx] += jnp.dot(vt, jnp.exp2(s).astype(BF16), preferred_element_type=F32)

    def run(work):
        pending = [scores(*w[:3]) for w in work[:depth]]
        for t, (h, j, kb, vt, diag) in enumerate(work):
            if t + depth < len(work):
                pending.append(scores(*work[t + depth][:3]))
            update(h, j, pending.pop(0), vt, diag)

    for h in range(2):
        def body(kv, carry, h=h):
            kb = k_ref[h, pl.ds(pl.multiple_of(kv * kt, kt), kt), :]
            run([(h, j, kb, vt_ref[h, kv], None) for j in range(nsub)])
            return carry

        first = 0
        if not online_max:
            head = 2 * pair + h
            cq = cfirst_ref[b, head, i]
            first = lax.while_loop(
                lambda kv: jnp.logical_and(kv < i * per, cq - clast_ref[b, head, kv] < -SKIP_LOG2),
                lambda kv: kv + 1, 0)
        lax.fori_loop(first, i * per, body, 0)

        work = []
        for jk in range(per):
            kb = k_ref[h, pl.ds(pl.multiple_of((i * per + jk) * kt, kt), kt), :]
            vt = vt_ref[h, i * per + jk]
            for j in range(nsub):
                diag = jk * kt - j * sub
                if diag >= sub:
                    continue
                work.append((h, j, kb, vt, diag if diag + kt - 1 > 0 else None))
        run(work)

    for j in range(nsub):
        outs = []
        for h in range(2):
            acc = acc_ref[h * nsub + j]
            outs.append(acc[0:HEAD_DIM, :] * (1.0 / acc[HEAD_DIM:HEAD_DIM + 1, :]))
        pair_t = jnp.concatenate(outs, axis=0)
        o_ref[j * sub:(j + 1) * sub, :] = pair_t.T.astype(o_ref.dtype)


def _fox(cfirst, clast, q_aug, k_aug, vt, tq, sub, online_max):
    bsz, n_heads, seq, _ = q_aug.shape
    nkt, kt = vt.shape[2], vt.shape[4]
    npair = n_heads // 2
    nsub = tq // sub
    return pl.pallas_call(
        functools.partial(_fox_kernel, tq=tq, kt=kt, sub=sub, depth=3, online_max=online_max),
        grid=(bsz, npair, seq // tq),
        in_specs=[
            pl.BlockSpec(memory_space=pltpu.SMEM),
            pl.BlockSpec(memory_space=pltpu.SMEM),
            pl.BlockSpec((None, 2, tq, LANES), lambda b, p, i: (b, p, i, 0)),
            pl.BlockSpec((None, 2, seq, LANES), lambda b, p, i: (b, p, 0, 0)),
            pl.BlockSpec((None, 2, nkt, VT_ROWS, kt), lambda b, p, i: (b, p, 0, 0, 0)),
        ],
        out_specs=pl.BlockSpec((None, tq, LANES), lambda b, p, i: (b, i, p)),
        out_shape=jax.ShapeDtypeStruct((bsz, seq, n_heads * HEAD_DIM), BF16),
        scratch_shapes=[pltpu.VMEM((2 * nsub, 1, sub), F32), pltpu.VMEM((2 * nsub, VT_ROWS, sub), F32)],
        compiler_params=_cparams(("parallel", "parallel", "arbitrary")),
        name="fox_online_max" if online_max else "fox",
    )(cfirst, clast, q_aug, k_aug, vt).reshape(bsz * seq, n_heads * HEAD_DIM)


def _pair_gain(gain, scale=1.0):
    return jnp.tile(gain.astype(F32) * scale, 2)


def _forward(x, g_mix, g_mlp, w_in_even, a_q_gain, a_k_gain, b_q_gain, b_k_gain, b_sinks,
             w_out_even, w_in_odd, b_forget, c_q_gain, c_k_gain, w_out_odd, w_up, w_down):
    bsz, seq, d = x.shape
    n = bsz * seq
    depth = g_mix.shape[0]
    scale = HEAD_DIM ** -0.5
    n_heads_a = d // (2 * HEAD_DIM)
    n_heads_b = d // (2 * HEAD_DIM)
    n_heads_c = d // HEAD_DIM
    slopes = _alibi_slopes(n_heads_a + n_heads_b)
    x2 = x.reshape(n, d)
    tm = 512

    for layer in range(depth):
        i = layer // 2
        g1 = g_mix[layer].reshape(1, d)
        g2 = g_mlp[layer].reshape(1, d)
        wu = w_up[layer].astype(BF16)
        wd = w_down[layer].astype(BF16)
        if layer % 2 == 0:
            gains = jnp.stack([_pair_gain(a_q_gain[i], scale), _pair_gain(a_k_gain[i]),
                               _pair_gain(b_q_gain[i], scale), _pair_gain(b_k_gain[i])])
            nat, by4, by16 = _in_even(x2, g1, w_in_even[i].astype(BF16), gains, seq, tm)
            nat = nat.reshape(bsz, 1, seq, nat.shape[1])
            dilated = dict(groups=(0, 1, 2), kcols=(0, 128, 256, 384), vcols=(0, 128, 256, 384),
                           max_dist=BLOCK, slopes=slopes[n_heads_b:])
            part = _banded(by16, dil=16, emit_lse=True, **dilated)
            part = _banded(by4, dil=4, prev=part, emit_lse=True, **dilated)
            a_out = _banded(nat, dil=1, prev=part, **dilated)
            b_out = _banded(nat, dil=1, groups=(3, 4, 4), kcols=(0, 0, 128, 128), vcols=(256, 256, 384, 384),
                            max_dist=WINDOW_B - 1, slopes=slopes[:n_heads_b], sinks=b_sinks[i].astype(F32))
            ys = [a_out.reshape(n, 512), b_out.reshape(n, 512)]
            x2 = _post(ys, x2, w_out_even[i].astype(BF16), g2, wu, wd, 256)
        else:
            w = w_in_odd[i]
            w_qkf = jnp.pad(jnp.concatenate([w[:, :2 * d], w[:, 3 * d:]], axis=1),
                            ((0, 0), (0, LANES - n_heads_c))).astype(BF16)
            w_vt = w[:, 2 * d:3 * d].T.astype(BF16)
            bound = (HEAD_DIM * scale * LOG2E) * jnp.max(jnp.abs(c_q_gain[i])) * jnp.max(jnp.abs(c_k_gain[i]))
            shift = (bound.astype(F32) * 1.02).astype(BF16).astype(F32)
            gains = jnp.stack([_pair_gain(c_q_gain[i], scale * LOG2E), _pair_gain(c_k_gain[i])])
            bf = jnp.pad(b_forget[i].astype(F32), (0, LANES - n_heads_c)).reshape(1, LANES)
            eq, ek = _spare_lane_placement(n_heads_c, shift)
            q_aug, k_aug, vt, cfirst, clast = _in_odd(x2, g1, w_qkf, w_vt, eq, ek, gains, bf, seq, tm, n_heads_c)
            tq = min(FOX_TQ, seq)
            cfirst = cfirst.reshape(bsz, seq // tm, LANES)[:, ::tq // tm, :n_heads_c].transpose(0, 2, 1)
            clast = clast.reshape(bsz, seq // tm, LANES)[:, :, :n_heads_c].transpose(0, 2, 1)
            fox = functools.partial(_fox, tq=tq, sub=256)
            y = lax.cond(shift <= MAX_FIXED_SHIFT,
                         lambda *a: fox(*a, online_max=False),
                         lambda *a: fox(*a, online_max=True),
                         cfirst, clast, q_aug, k_aug, vt)
            x2 = _post([y], x2, w_out_odd[i].astype(BF16), g2, wu, wd, 256)
    return x2.reshape(bsz, seq, d)


def kernel(x, g_mix, g_mlp, w_in_even, a_q_gain, a_k_gain, b_q_gain, b_k_gain, b_sinks, w_out_even,
           w_in_odd, b_forget, c_q_gain, c_k_gain, w_out_odd, w_up, w_down):
    return _forward(x, g_mix, g_mlp, w_in_even, a_q_gain, a_k_gain, b_q_gain, b_k_gain, b_sinks,
                    w_out_even, w_in_odd, b_forget, c_q_gain, c_k_gain, w_out_odd, w_up, w_down)
```

```python
import functools

import numpy as np
import jax
import jax.numpy as jnp
from jax import lax
from jax.experimental import pallas as pl
from jax.experimental.pallas import tpu as pltpu

F32 = jnp.float32
BF16 = jnp.bfloat16

HEAD_DIM = 64
LANES = 128
EPS = 1e-6
NEG = -1e30
BLOCK = 128
DILATIONS = (1, 4, 16)
WINDOW_B = 128
LOG2E = float(np.log2(np.e))
VT_ROWS = 80
ROW_GROUPS = 2
FOX_TQ = 2048
SKIP_LOG2 = 160.0
MAX_FIXED_SHIFT = 40.0
VMEM_LIMIT = 56 * 1024 * 1024


def _alibi_slopes(n):
    return np.asarray(2.0 ** (-8.0 * np.arange(1, n + 1) / n), dtype=np.float32)


def _cparams(sem):
    return pltpu.CompilerParams(dimension_semantics=sem, vmem_limit_bytes=VMEM_LIMIT)


def _rms(x, gain):
    return x * lax.rsqrt(jnp.mean(x * x, axis=-1, keepdims=True) + EPS) * gain


def _lo_mask(shape):
    return lax.broadcasted_iota(jnp.int32, shape, len(shape) - 1) < HEAD_DIM


def _pair_norm(xc, gain2):
    lo = _lo_mask(xc.shape)
    sq = xc * xc
    s_lo = jnp.sum(jnp.where(lo, sq, 0.0), axis=-1, keepdims=True)
    s_hi = jnp.sum(jnp.where(lo, 0.0, sq), axis=-1, keepdims=True)
    r = jnp.where(lo, lax.rsqrt(s_lo / HEAD_DIM + EPS), lax.rsqrt(s_hi / HEAD_DIM + EPS))
    return xc * r * gain2


def _in_even_kernel(x_ref, g_ref, w_ref, gains_ref, o_ref, o4_ref, o16_ref, s1_ref, s2_ref):
    tm = x_ref.shape[0]
    rt = tm // ROW_GROUPS
    for grp in range(ROW_GROUPS):
        _in_even_rows(grp, rt, x_ref, g_ref, w_ref, gains_ref, o_ref, o4_ref, o16_ref, s1_ref, s2_ref)


def _in_even_rows(grp, rt, x_ref, g_ref, w_ref, gains_ref, o_ref, o4_ref, o16_ref, s1_ref, s2_ref):
    rows = slice(grp * rt, (grp + 1) * rt)
    n4, n16 = rt // 4, rt // 16
    h = _rms(x_ref[rows, :], g_ref[...]).astype(BF16)
    proj = jnp.dot(h, w_ref[...], preferred_element_type=F32)
    lo = _lo_mask((rt, LANES))

    def col(c):
        return proj[:, c * LANES:(c + 1) * LANES]

    def put(c, val):
        o_ref[rows, c * LANES:(c + 1) * LANES] = val.astype(o_ref.dtype)
        if c >= 12:
            return
        cols = slice(c * LANES, (c + 1) * LANES)
        s1_ref[grp, c] = val
        for r in range(4):
            part = s1_ref[grp, c, pl.ds(r, n4, stride=4), :]
            o4_ref[r, grp * n4:(grp + 1) * n4, cols] = part.astype(o4_ref.dtype)
            s2_ref[grp, c, r * n4:(r + 1) * n4, :] = part
        for r in range(4):
            for r2 in range(4):
                part = s2_ref[grp, c, pl.ds(r * n4 + r2, n16, stride=4), :]
                o16_ref[4 * r + r2, grp * n16:(grp + 1) * n16, cols] = part.astype(o16_ref.dtype)

    for c in range(4):
        put(c, _pair_norm(col(c), gains_ref[0:1, :]))
    for c in range(4, 8):
        put(c, _pair_norm(col(c), gains_ref[1:2, :]))
    for c in range(8, 12):
        put(c, col(c))
    for c in range(12, 16):
        put(c, _pair_norm(col(c), gains_ref[2:3, :]))
    bk = _pair_norm(col(16), gains_ref[3:4, :])
    bkr = pltpu.roll(bk, HEAD_DIM, 1)
    put(16, jnp.where(lo, bk, bkr))
    put(17, jnp.where(lo, bkr, bk))
    bv = col(17)
    bvr = pltpu.roll(bv, HEAD_DIM, 1)
    put(18, jnp.where(lo, bv, bvr))
    put(19, jnp.where(lo, bvr, bv))


def _in_even(x2, g, w, gains, seq, tm):
    n, d = x2.shape
    e = w.shape[1]
    bsz = n // seq
    tiles_per_seq = seq // tm
    a_width = 12 * LANES

    def regrouped(dil):
        return pl.BlockSpec((None, dil, tm // dil, a_width),
                            lambda i: (i // tiles_per_seq, 0, i % tiles_per_seq, 0))

    return pl.pallas_call(
        _in_even_kernel,
        grid=(n // tm,),
        in_specs=[
            pl.BlockSpec((tm, d), lambda i: (i, 0)),
            pl.BlockSpec((1, d), lambda i: (0, 0)),
            pl.BlockSpec((d, e), lambda i: (0, 0)),
            pl.BlockSpec((4, LANES), lambda i: (0, 0)),
        ],
        out_specs=[pl.BlockSpec((tm, 20 * LANES), lambda i: (i, 0)), regrouped(4), regrouped(16)],
        out_shape=[jax.ShapeDtypeStruct((n, 20 * LANES), BF16),
                   jax.ShapeDtypeStruct((bsz, 4, seq // 4, a_width), BF16),
                   jax.ShapeDtypeStruct((bsz, 16, seq // 16, a_width), BF16)],
        scratch_shapes=[pltpu.VMEM((ROW_GROUPS, 12, tm // ROW_GROUPS, LANES), F32)] * 2,
        compiler_params=_cparams(("parallel",)),
        name="in_even",
    )(x2, g, w, gains)


def _banded_kernel(*refs, tu, dil, max_dist, slopes, kcols, vcols, use_sink, merge_in, emit_lse,
                   lookahead=4):
    refs = list(refs)
    sink_ref = refs.pop(0) if use_sink else None
    q_ref, kc_ref, kp_ref, vc_ref, vp_ref = refs[:5]
    refs = refs[5:]
    if merge_in:
        po_ref, pl_ref = refs[:2]
        refs = refs[2:]
    o_ref = refs.pop(0)
    lse_ref = refs.pop(0) if emit_lse else None
    if merge_in:
        mo_ref, ml_ref = refs
        n4 = tu // 4
        for slab in range(4):
            cs = slice(slab * LANES, (slab + 1) * LANES)
            for r2 in range(4):
                mo_ref[slab, pl.ds(r2, n4, stride=4), :] = po_ref[r2, :, cs]
                ml_ref[slab, pl.ds(r2, n4, stride=4), :] = pl_ref[r2, :, cs]
    first = pl.program_id(2) == 0
    nj = tu // BLOCK
    lo = _lo_mask((BLOCK, LANES))
    qi = lax.broadcasted_iota(jnp.int32, (BLOCK, 2 * BLOCK), 0)
    kj = lax.broadcasted_iota(jnp.int32, (BLOCK, 2 * BLOCK), 1)
    dist = qi + BLOCK - kj
    valid = jnp.logical_and(dist >= 0, dist <= max_dist)
    valid_first = jnp.logical_and(valid, jnp.logical_or(kj >= BLOCK, jnp.logical_not(first)))
    distf = (dist * dil).astype(F32)

    n_heads = 2 * len(kcols)
    biases = []
    for head in range(n_heads):
        bias = -slopes[head] * distf
        biases.append((jnp.where(valid_first, bias, NEG), jnp.where(valid, bias, NEG)))

    def window(cur_ref, halo_ref, j, c0):
        if j == 0:
            return jnp.concatenate([halo_ref[:, c0:c0 + LANES], cur_ref[0:BLOCK, c0:c0 + LANES]], axis=0)
        return cur_ref[(j - 1) * BLOCK:(j + 1) * BLOCK, c0:c0 + LANES]

    def scores(hp, j, half):
        q2 = q_ref[j * BLOCK:(j + 1) * BLOCK, hp * LANES:(hp + 1) * LANES]
        qm = jnp.where(lo, q2, 0) if half == 0 else jnp.where(lo, 0, q2)
        kw = window(kc_ref, kp_ref, j, kcols[hp])
        return lax.dot_general(qm, kw, (((1,), (1,)), ((), ())), preferred_element_type=F32)

    halves = {}

    def finish(hp, j, half, s):
        s = s + biases[2 * hp + half][0 if j == 0 else 1]
        m = jnp.max(s, axis=-1, keepdims=True)
        if use_sink:
            sink = sink_ref[2 * hp + half]
            m = jnp.maximum(m, sink)
        p = jnp.exp(s - m)
        l = jnp.sum(p, axis=-1, keepdims=True)
        if use_sink:
            l = l + jnp.exp(sink - m)
        vw = window(vc_ref, vp_ref, j, vcols[hp])
        pv = jnp.dot(p.astype(BF16), vw, preferred_element_type=F32)
        halves[half] = (pv * (1.0 / l), m + jnp.log(l))
        if half == 0:
            return
        rows = slice(j * BLOCK, (j + 1) * BLOCK)
        cols = slice(hp * LANES, (hp + 1) * LANES)
        o_blk = jnp.where(lo, halves[0][0], halves[1][0])
        l_blk = jnp.where(lo, halves[0][1], halves[1][1])
        if merge_in:
            o_prev, l_prev = mo_ref[hp, rows, :], ml_ref[hp, rows, :]
            mx = jnp.maximum(l_blk, l_prev)
            w_own, w_prev = jnp.exp(l_blk - mx), jnp.exp(l_prev - mx)
            den = w_own + w_prev
            o_blk = (w_own * o_blk + w_prev * o_prev) * (1.0 / den)
            l_blk = mx + jnp.log(den)
        o_ref[rows, cols] = o_blk.astype(o_ref.dtype)
        if emit_lse:
            lse_ref[rows, cols] = l_blk

    items = [(hp, j, half) for hp in range(len(kcols)) for j in range(nj) for half in range(2)]
    pending = [scores(*it) for it in items[:lookahead]]
    for t, it in enumerate(items):
        if t + lookahead < len(items):
            pending.append(scores(*items[t + lookahead]))
        finish(*it, pending.pop(0))


def _banded(src, *, dil, groups, kcols, vcols, max_dist, slopes, sinks=None, prev=None, emit_lse=False):
    bsz, _, u, width = src.shape
    tu = min(512, u)
    per = tu // BLOCK

    def spec(grp, halo):
        if halo:
            return pl.BlockSpec((None, None, BLOCK, 512),
                                lambda b, r, i: (b, r, jnp.maximum(i * per - 1, 0), grp))
        return pl.BlockSpec((None, None, tu, 512), lambda b, r, i: (b, r, i, grp))

    qg, kg, vg = groups
    in_specs = [spec(qg, False), spec(kg, False), spec(kg, True), spec(vg, False), spec(vg, True)]
    args = [src] * 5
    use_sink = sinks is not None
    if use_sink:
        in_specs = [pl.BlockSpec(memory_space=pltpu.SMEM)] + in_specs
        args = [sinks] + args
    scratch = []
    if prev is not None:
        in_specs += [pl.BlockSpec((None, 4, tu // 4, 512), lambda b, r, i: (b, r, i, 0))] * 2
        args += list(prev)
        scratch = [pltpu.VMEM((4, tu, LANES), F32)] * 2
    out_spec = pl.BlockSpec((None, None, tu, 512), lambda b, r, i: (b, r, i, 0))
    if emit_lse:
        out_specs = [out_spec, out_spec]
        out_shape = [jax.ShapeDtypeStruct((bsz, dil, u, 512), F32)] * 2
    else:
        out_specs = out_spec
        out_shape = jax.ShapeDtypeStruct((bsz, dil, u, 512), BF16)
    kern = functools.partial(_banded_kernel, tu=tu, dil=dil, max_dist=max_dist,
                             slopes=tuple(float(s) for s in slopes), kcols=kcols, vcols=vcols,
                             use_sink=use_sink, merge_in=prev is not None, emit_lse=emit_lse)
    return pl.pallas_call(
        kern,
        grid=(bsz, dil, u // tu),
        in_specs=in_specs,
        out_specs=out_specs,
        out_shape=out_shape,
        scratch_shapes=scratch,
        compiler_params=_cparams(("parallel", "parallel", "arbitrary")),
        name=f"banded_d{dil}" + ("_sink" if use_sink else ""),
    )(*args)


def _post_kernel(*refs, n_y, ff_chunk):
    y_refs = refs[:n_y]
    x_ref, wo_ref, g_ref, wu_ref, wd_ref, out_ref = refs[n_y:]
    y = y_refs[0][...] if n_y == 1 else jnp.concatenate([r[...] for r in y_refs], axis=1)
    x1 = x_ref[...] + jnp.dot(y, wo_ref[...], preferred_element_type=F32)
    h = _rms(x1, g_ref[...]).astype(BF16)
    acc = x1
    d_ff = wu_ref.shape[1]
    for f in range(d_ff // ff_chunk):
        cs = slice(f * ff_chunk, (f + 1) * ff_chunk)
        u = jnp.maximum(jnp.dot(h, wu_ref[:, cs], preferred_element_type=F32), 0.0)
        acc = acc + jnp.dot((u * u).astype(BF16), wd_ref[cs, :], preferred_element_type=F32)
    out_ref[...] = acc


def _post(ys, x2, w_out, g, w_up, w_down, tm):
    n, d = x2.shape
    d_ff = w_up.shape[1]

    def rows(width):
        return pl.BlockSpec((tm, width), lambda i: (i, 0))

    def whole(shape):
        return pl.BlockSpec(shape, lambda i: (0, 0), pipeline_mode=pl.Buffered(1))

    in_specs = [rows(y.shape[1]) for y in ys] + [
        rows(d), whole((d, d)), whole((1, d)), whole((d, d_ff)), whole((d_ff, d))]
    return pl.pallas_call(
        functools.partial(_post_kernel, n_y=len(ys), ff_chunk=1024),
        grid=(n // tm,),
        in_specs=in_specs,
        out_specs=rows(d),
        out_shape=jax.ShapeDtypeStruct((n, d), F32),
        compiler_params=_cparams(("parallel",)),
        name=f"post_{len(ys)}",
    )(*ys, x2, w_out, g, w_up, w_down)


def _split3(x):
    hi = x.astype(BF16).astype(F32)
    r = x - hi
    mid = r.astype(BF16).astype(F32)
    return hi, mid, r - mid


def _in_odd_kernel(x_ref, g_ref, w_ref, wvt_ref, eq_ref, ek_ref, gains_ref, bf_ref,
                   qa_ref, ka_ref, vt_ref, cfirst_ref, clast_ref, carry_ref, *, tiles_per_seq):
    i = pl.program_id(0)
    tm = x_ref.shape[0]
    rt = tm // ROW_GROUPS
    d2 = w_ref.shape[1] - LANES
    npair = d2 // (2 * LANES)

    @pl.when(i % tiles_per_seq == 0)
    def _():
        carry_ref[...] = jnp.zeros_like(carry_ref)

    r = lax.broadcasted_iota(jnp.int32, (rt, rt), 0)
    c = lax.broadcasted_iota(jnp.int32, (rt, rt), 1)
    tri = (c <= r).astype(F32)
    lane = lax.broadcasted_iota(jnp.int32, (rt, LANES), 1)
    lo = lane < HEAD_DIM
    ones_rows = (lax.broadcasted_iota(jnp.int32, (VT_ROWS - HEAD_DIM, rt), 0) == 0).astype(vt_ref.dtype)

    for grp in range(ROW_GROUPS):
        rows = slice(grp * rt, (grp + 1) * rt)
        h = _rms(x_ref[rows, :], g_ref[...]).astype(BF16)
        z = jnp.dot(h, w_ref[:, d2:], preferred_element_type=F32) + bf_ref[...]
        proj = jnp.dot(h, w_ref[:, :d2], preferred_element_type=F32)
        vt_all = lax.dot_general(wvt_ref[...], h, (((1,), (1,)), ((), ())),
                                 preferred_element_type=F32)
        log_f = jnp.minimum(z, 0.0) - jnp.log1p(jnp.exp(-jnp.abs(z)))
        local = jnp.dot(tri, log_f, preferred_element_type=F32, precision=lax.Precision.HIGHEST)
        csum = local + carry_ref[0:1, :]
        carry_ref[0:1, :] = csum[rt - 1:rt, :]
        c2 = csum * LOG2E
        if grp == 0:
            cfirst_ref[...] = c2[0:1, :]
        if grp == ROW_GROUPS - 1:
            clast_ref[...] = c2[rt - 1:rt, :]

        hi, mid, low = _split3(c2)
        a = jnp.where(lane < 16, hi, jnp.where(lane < 32, pltpu.roll(mid, 16, 1),
                      jnp.where(lane < 48, pltpu.roll(low, 32, 1), jnp.where(lane == 48, 1.0, 0.0))))
        a = a.astype(BF16)
        q_spare = jnp.dot(a, eq_ref[...], preferred_element_type=F32)
        k_spare = jnp.dot(a, ek_ref[...], preferred_element_type=F32)

        for hp in range(npair):
            qp = _pair_norm(proj[:, hp * LANES:(hp + 1) * LANES], gains_ref[0:1, :])
            kp = _pair_norm(proj[:, (npair + hp) * LANES:(npair + hp + 1) * LANES], gains_ref[1:2, :])
            qs = (qp, pltpu.roll(qp, HEAD_DIM, 1))
            ks = (kp, pltpu.roll(kp, HEAD_DIM, 1))
            for half in range(2):
                hd = 2 * hp + half
                cols = slice(hd * LANES, (hd + 1) * LANES)
                ka_ref[hd, rows, :] = jnp.where(lo, ks[half], k_spare[:, cols]).astype(ka_ref.dtype)
                qa_ref[hd, rows, :] = jnp.where(lo, qs[half], q_spare[:, cols]).astype(qa_ref.dtype)

        for hd in range(2 * npair):
            vt_ref[hd, 0:HEAD_DIM, rows] = vt_all[hd * HEAD_DIM:(hd + 1) * HEAD_DIM, :].astype(vt_ref.dtype)
            vt_ref[hd, HEAD_DIM:VT_ROWS, rows] = ones_rows


def _spare_lane_placement(n_heads, shift):
    eq = np.zeros((LANES, n_heads * LANES), np.float32)
    ek = np.zeros((LANES, n_heads * LANES), np.float32)
    sh = np.zeros((LANES, n_heads * LANES), np.float32)
    for h in range(n_heads):
        base = h * LANES + HEAD_DIM
        for part in range(3):
            ek[16 * part + h, base + part] = 1.0
            eq[16 * part + h, base + 3 + part] = 1.0
        eq[48, base:base + 3] = -1.0
        ek[48, base + 3:base + 7] = 1.0
        sh[48, base + 6] = -1.0
    return (jnp.asarray(eq) + jnp.asarray(sh) * shift).astype(BF16), jnp.asarray(ek).astype(BF16)


def _in_odd(x2, g, w_qkf, w_vt, eq, ek, gains, bf, seq, tm, n_heads):
    n, d = x2.shape
    tiles_per_seq = seq // tm
    bsz = n // seq

    def bt(i):
        return i // tiles_per_seq, i % tiles_per_seq

    def whole(a):
        return pl.BlockSpec(a.shape, lambda i: (0, 0))

    return pl.pallas_call(
        functools.partial(_in_odd_kernel, tiles_per_seq=tiles_per_seq),
        grid=(n // tm,),
        in_specs=[pl.BlockSpec((tm, d), lambda i: (i, 0)), whole(g), whole(w_qkf), whole(w_vt),
                  whole(eq), whole(ek), whole(gains), whole(bf)],
        out_specs=[
            pl.BlockSpec((None, n_heads, tm, LANES), lambda i: (bt(i)[0], 0, bt(i)[1], 0)),
            pl.BlockSpec((None, n_heads, tm, LANES), lambda i: (bt(i)[0], 0, bt(i)[1], 0)),
            pl.BlockSpec((None, n_heads, None, VT_ROWS, tm), lambda i: (bt(i)[0], 0, bt(i)[1], 0, 0)),
            pl.BlockSpec((None, 1, LANES), lambda i: (i, 0, 0)),
            pl.BlockSpec((None, 1, LANES), lambda i: (i, 0, 0)),
        ],
        out_shape=[
            jax.ShapeDtypeStruct((bsz, n_heads, seq, LANES), BF16),
            jax.ShapeDtypeStruct((bsz, n_heads, seq, LANES), BF16),
            jax.ShapeDtypeStruct((bsz, n_heads, tiles_per_seq, VT_ROWS, tm), BF16),
            jax.ShapeDtypeStruct((n // tm, 1, LANES), F32),
            jax.ShapeDtypeStruct((n // tm, 1, LANES), F32),
        ],
        scratch_shapes=[pltpu.VMEM((8, LANES), F32)],
        compiler_params=_cparams(("arbitrary",)),
        name="in_odd",
    )(x2, g, w_qkf, w_vt, eq, ek, gains, bf)


def _fox_kernel(cfirst_ref, clast_ref, q_ref, k_ref, vt_ref, o_ref, m_ref, acc_ref,
                *, tq, kt, sub, depth, online_max):
    assert kt % sub == 0 and tq % kt == 0
    b = pl.program_id(0)
    pair = pl.program_id(1)
    i = pl.program_id(2)
    nsub = tq // sub
    per = tq // kt
    m_ref[...] = jnp.full(m_ref.shape, NEG, F32)
    acc_ref[...] = jnp.zeros(acc_ref.shape, F32)
    krow = lax.broadcasted_iota(jnp.int32, (kt, sub), 0)
    qcol = lax.broadcasted_iota(jnp.int32, (kt, sub), 1)

    def scores(h, j, kb):
        qs = q_ref[h, j * sub:(j + 1) * sub, :]
        return lax.dot_general(kb, qs, (((1,), (1,)), ((), ())), preferred_element_type=F32)

    def update(h, j, s, vt, diag):
        idx = h * nsub + j
        if diag is not None:
            s = jnp.where(krow + diag <= qcol, s, NEG)
        if online_max:
            m_old = m_ref[idx]
            m_new = jnp.maximum(m_old, jnp.max(s, axis=0, keepdims=True))
            alpha = jnp.exp2(m_old - m_new)
            p = jnp.exp2(s - m_new).astype(BF16)
            m_ref[idx] = m_new
            acc_ref[idx] = alpha * acc_ref[idx] + jnp.dot(vt, p, preferred_element_type=F32)
        else:
            acc_ref[idx] += jnp.dot(vt, jnp.exp2(s).astype(BF16), preferred_element_type=F32)

    def run(work):
        pending = [scores(*w[:3]) for w in work[:depth]]
        for t, (h, j, kb, vt, diag) in enumerate(work):
            if t + depth < len(work):
                pending.append(scores(*work[t + depth][:3]))
            update(h, j, pending.pop(0), vt, diag)

    def key_block(h, blk):
        return k_ref[h, pl.ds(pl.multiple_of(blk * kt, kt), kt), :]

    def body(kv, carry):
        run([(h, j, key_block(h, kv), vt_ref[h, kv], None) for h in range(2) for j in range(nsub)])
        return carry

    first = 0
    if not online_max:
        def negligible(kv):
            gaps = [cfirst_ref[b, 2 * pair + h, i] - clast_ref[b, 2 * pair + h, kv] for h in range(2)]
            return jnp.maximum(gaps[0], gaps[1]) < -SKIP_LOG2

        first = lax.while_loop(lambda kv: jnp.logical_and(kv < i * per, negligible(kv)),
                               lambda kv: kv + 1, 0)
    lax.fori_loop(first, i * per, body, 0)

    work = []
    for h in range(2):
        for jk in range(per):
            kb = key_block(h, i * per + jk)
            vt = vt_ref[h, i * per + jk]
            for j in range(nsub):
                diag = jk * kt - j * sub
                if diag >= sub:
                    continue
                work.append((h, j, kb, vt, diag if diag + kt - 1 > 0 else None))
    run(work)

    for j in range(nsub):
        outs = []
        for h in range(2):
            acc = acc_ref[h * nsub + j]
            outs.append(acc[0:HEAD_DIM, :] * (1.0 / acc[HEAD_DIM:HEAD_DIM + 1, :]))
        pair_t = jnp.concatenate(outs, axis=0)
        o_ref[j * sub:(j + 1) * sub, :] = pair_t.T.astype(o_ref.dtype)


def _fox(cfirst, clast, q_aug, k_aug, vt, tq, sub, online_max):
    bsz, n_heads, seq, _ = q_aug.shape
    nkt, kt = vt.shape[2], vt.shape[4]
    npair = n_heads // 2
    nsub = tq // sub
    return pl.pallas_call(
        functools.partial(_fox_kernel, tq=tq, kt=kt, sub=sub, depth=4, online_max=online_max),
        grid=(bsz, npair, seq // tq),
        in_specs=[
            pl.BlockSpec(memory_space=pltpu.SMEM),
            pl.BlockSpec(memory_space=pltpu.SMEM),
            pl.BlockSpec((None, 2, tq, LANES), lambda b, p, i: (b, p, i, 0)),
            pl.BlockSpec((None, 2, seq, LANES), lambda b, p, i: (b, p, 0, 0)),
            pl.BlockSpec((None, 2, nkt, VT_ROWS, kt), lambda b, p, i: (b, p, 0, 0, 0)),
        ],
        out_specs=pl.BlockSpec((None, tq, LANES), lambda b, p, i: (b, i, p)),
        out_shape=jax.ShapeDtypeStruct((bsz, seq, n_heads * HEAD_DIM), BF16),
        scratch_shapes=[pltpu.VMEM((2 * nsub, 1, sub), F32), pltpu.VMEM((2 * nsub, VT_ROWS, sub), F32)],
        compiler_params=_cparams(("parallel", "parallel", "arbitrary")),
        name="fox_online_max" if online_max else "fox",
    )(cfirst, clast, q_aug, k_aug, vt).reshape(bsz * seq, n_heads * HEAD_DIM)


def _pair_gain(gain, scale=1.0):
    return jnp.tile(gain.astype(F32) * scale, 2)


def _forward(x, g_mix, g_mlp, w_in_even, a_q_gain, a_k_gain, b_q_gain, b_k_gain, b_sinks,
             w_out_even, w_in_odd, b_forget, c_q_gain, c_k_gain, w_out_odd, w_up, w_down):
    bsz, seq, d = x.shape
    n = bsz * seq
    depth = g_mix.shape[0]
    scale = HEAD_DIM ** -0.5
    n_heads_a = d // (2 * HEAD_DIM)
    n_heads_b = d // (2 * HEAD_DIM)
    n_heads_c = d // HEAD_DIM
    slopes = _alibi_slopes(n_heads_a + n_heads_b)
    x2 = x.reshape(n, d)
    tm = 512

    for layer in range(depth):
        i = layer // 2
        g1 = g_mix[layer].reshape(1, d)
        g2 = g_mlp[layer].reshape(1, d)
        wu = w_up[layer].astype(BF16)
        wd = w_down[layer].astype(BF16)
        if layer % 2 == 0:
            gains = jnp.stack([_pair_gain(a_q_gain[i], scale), _pair_gain(a_k_gain[i]),
                               _pair_gain(b_q_gain[i], scale), _pair_gain(b_k_gain[i])])
            nat, by4, by16 = _in_even(x2, g1, w_in_even[i].astype(BF16), gains, seq, tm)
            nat = nat.reshape(bsz, 1, seq, nat.shape[1])
            dilated = dict(groups=(0, 1, 2), kcols=(0, 128, 256, 384), vcols=(0, 128, 256, 384),
                           max_dist=BLOCK, slopes=slopes[n_heads_b:])
            part = _banded(by16, dil=16, emit_lse=True, **dilated)
            part = _banded(by4, dil=4, prev=part, emit_lse=True, **dilated)
            a_out = _banded(nat, dil=1, prev=part, **dilated)
            b_out = _banded(nat, dil=1, groups=(3, 4, 4), kcols=(0, 0, 128, 128), vcols=(256, 256, 384, 384),
                            max_dist=WINDOW_B - 1, slopes=slopes[:n_heads_b], sinks=b_sinks[i].astype(F32))
            ys = [a_out.reshape(n, 512), b_out.reshape(n, 512)]
            x2 = _post(ys, x2, w_out_even[i].astype(BF16), g2, wu, wd, 256)
        else:
            w = w_in_odd[i]
            w_qkf = jnp.pad(jnp.concatenate([w[:, :2 * d], w[:, 3 * d:]], axis=1),
                            ((0, 0), (0, LANES - n_heads_c))).astype(BF16)
            w_vt = w[:, 2 * d:3 * d].T.astype(BF16)
            bound = (HEAD_DIM * scale * LOG2E) * jnp.max(jnp.abs(c_q_gain[i])) * jnp.max(jnp.abs(c_k_gain[i]))
            shift = (bound.astype(F32) * 1.02).astype(BF16).astype(F32)
            gains = jnp.stack([_pair_gain(c_q_gain[i], scale * LOG2E), _pair_gain(c_k_gain[i])])
            bf = jnp.pad(b_forget[i].astype(F32), (0, LANES - n_heads_c)).reshape(1, LANES)
            eq, ek = _spare_lane_placement(n_heads_c, shift)
            q_aug, k_aug, vt, cfirst, clast = _in_odd(x2, g1, w_qkf, w_vt, eq, ek, gains, bf, seq, tm, n_heads_c)
            tq = min(FOX_TQ, seq)
            cfirst = cfirst.reshape(bsz, seq // tm, LANES)[:, ::tq // tm, :n_heads_c].transpose(0, 2, 1)
            clast = clast.reshape(bsz, seq // tm, LANES)[:, :, :n_heads_c].transpose(0, 2, 1)
            fox = functools.partial(_fox, tq=tq, sub=256)
            y = lax.cond(shift <= MAX_FIXED_SHIFT,
                         lambda *a: fox(*a, online_max=False),
                         lambda *a: fox(*a, online_max=True),
                         cfirst, clast, q_aug, k_aug, vt)
            x2 = _post([y], x2, w_out_odd[i].astype(BF16), g2, wu, wd, 256)
    return x2.reshape(bsz, seq, d)


def kernel(x, g_mix, g_mlp, w_in_even, a_q_gain, a_k_gain, b_q_gain, b_k_gain, b_sinks, w_out_even,
           w_in_odd, b_forget, c_q_gain, c_k_gain, w_out_odd, w_up, w_down):
    return _forward(x, g_mix, g_mlp, w_in_even, a_q_gain, a_k_gain, b_q_gain, b_k_gain, b_sinks,
                    w_out_even, w_in_odd, b_forget, c_q_gain, c_k_gain, w_out_odd, w_up, w_down)
```

```python
import functools

import numpy as np
import jax
import jax.numpy as jnp
from jax import lax
from jax.experimental import pallas as pl
from jax.experimental.pallas import tpu as pltpu

F32 = jnp.float32
BF16 = jnp.bfloat16

HEAD_DIM = 64
LANES = 128
EPS = 1e-6
NEG = -1e30
BLOCK = 128
DILATIONS = (1, 4, 16)
WINDOW_B = 128
LOG2E = float(np.log2(np.e))
VT_ROWS = 80
ROW_GROUPS = 2
FOX_TQ = 2048
SKIP_LOG2 = 160.0
MAX_FIXED_SHIFT = 40.0
VMEM_LIMIT = 56 * 1024 * 1024


def _alibi_slopes(n):
    return np.asarray(2.0 ** (-8.0 * np.arange(1, n + 1) / n), dtype=np.float32)


def _cparams(sem):
    return pltpu.CompilerParams(dimension_semantics=sem, vmem_limit_bytes=VMEM_LIMIT)


def _rms(x, gain):
    return x * lax.rsqrt(jnp.mean(x * x, axis=-1, keepdims=True) + EPS) * gain


def _lo_mask(shape):
    return lax.broadcasted_iota(jnp.int32, shape, len(shape) - 1) < HEAD_DIM


def _pair_norm(xc, gain2):
    lo = _lo_mask(xc.shape)
    sq = xc * xc
    s_lo = jnp.sum(jnp.where(lo, sq, 0.0), axis=-1, keepdims=True)
    s_hi = jnp.sum(jnp.where(lo, 0.0, sq), axis=-1, keepdims=True)
    r = jnp.where(lo, lax.rsqrt(s_lo / HEAD_DIM + EPS), lax.rsqrt(s_hi / HEAD_DIM + EPS))
    return xc * r * gain2


def _in_even_kernel(x_ref, g_ref, w_ref, gains_ref, o_ref, o4_ref, o16_ref, s1_ref, s2_ref):
    tm = x_ref.shape[0]
    rt = tm // ROW_GROUPS
    for grp in range(ROW_GROUPS):
        _in_even_rows(grp, rt, x_ref, g_ref, w_ref, gains_ref, o_ref, o4_ref, o16_ref, s1_ref, s2_ref)


def _in_even_rows(grp, rt, x_ref, g_ref, w_ref, gains_ref, o_ref, o4_ref, o16_ref, s1_ref, s2_ref):
    rows = slice(grp * rt, (grp + 1) * rt)
    n4, n16 = rt // 4, rt // 16
    h = _rms(x_ref[rows, :], g_ref[...]).astype(BF16)
    proj = jnp.dot(h, w_ref[...], preferred_element_type=F32)
    lo = _lo_mask((rt, LANES))

    def col(c):
        return proj[:, c * LANES:(c + 1) * LANES]

    def put(c, val):
        o_ref[rows, c * LANES:(c + 1) * LANES] = val.astype(o_ref.dtype)
        if c >= 12:
            return
        cols = slice(c * LANES, (c + 1) * LANES)
        s1_ref[grp, c] = val
        for r in range(4):
            part = s1_ref[grp, c, pl.ds(r, n4, stride=4), :]
            o4_ref[r, grp * n4:(grp + 1) * n4, cols] = part.astype(o4_ref.dtype)
            s2_ref[grp, c, r * n4:(r + 1) * n4, :] = part
        for r in range(4):
            for r2 in range(4):
                part = s2_ref[grp, c, pl.ds(r * n4 + r2, n16, stride=4), :]
                o16_ref[4 * r + r2, grp * n16:(grp + 1) * n16, cols] = part.astype(o16_ref.dtype)

    for c in range(4):
        put(c, _pair_norm(col(c), gains_ref[0:1, :]))
    for c in range(4, 8):
        put(c, _pair_norm(col(c), gains_ref[1:2, :]))
    for c in range(8, 12):
        put(c, col(c))
    for c in range(12, 16):
        put(c, _pair_norm(col(c), gains_ref[2:3, :]))
    bk = _pair_norm(col(16), gains_ref[3:4, :])
    bkr = pltpu.roll(bk, HEAD_DIM, 1)
    put(16, jnp.where(lo, bk, bkr))
    put(17, jnp.where(lo, bkr, bk))
    bv = col(17)
    bvr = pltpu.roll(bv, HEAD_DIM, 1)
    put(18, jnp.where(lo, bv, bvr))
    put(19, jnp.where(lo, bvr, bv))


def _in_even(x2, g, w, gains, seq, tm):
    n, d = x2.shape
    e = w.shape[1]
    bsz = n // seq
    tiles_per_seq = seq // tm
    a_width = 12 * LANES

    def regrouped(dil):
        return pl.BlockSpec((None, dil, tm // dil, a_width),
                            lambda i: (i // tiles_per_seq, 0, i % tiles_per_seq, 0))

    return pl.pallas_call(
        _in_even_kernel,
        grid=(n // tm,),
        in_specs=[
            pl.BlockSpec((tm, d), lambda i: (i, 0)),
            pl.BlockSpec((1, d), lambda i: (0, 0)),
            pl.BlockSpec((d, e), lambda i: (0, 0)),
            pl.BlockSpec((4, LANES), lambda i: (0, 0)),
        ],
        out_specs=[pl.BlockSpec((tm, 20 * LANES), lambda i: (i, 0)), regrouped(4), regrouped(16)],
        out_shape=[jax.ShapeDtypeStruct((n, 20 * LANES), BF16),
                   jax.ShapeDtypeStruct((bsz, 4, seq // 4, a_width), BF16),
                   jax.ShapeDtypeStruct((bsz, 16, seq // 16, a_width), BF16)],
        scratch_shapes=[pltpu.VMEM((ROW_GROUPS, 12, tm // ROW_GROUPS, LANES), F32)] * 2,
        compiler_params=_cparams(("parallel",)),
        name="in_even",
    )(x2, g, w, gains)


def _banded_kernel(*refs, tu, dil, max_dist, slopes, kcols, vcols, use_sink, merge_in, emit_lse,
                   lookahead=4):
    refs = list(refs)
    scal_ref, q_ref, kc_ref, kp_ref, vc_ref, vp_ref = refs[:6]
    refs = refs[6:]
    if merge_in:
        po_ref, pl_ref = refs[:2]
        refs = refs[2:]
    o_ref = refs.pop(0)
    lse_ref = refs.pop(0) if emit_lse else None
    if merge_in:
        mo_ref, ml_ref = refs
        n4 = tu // 4
        for slab in range(4):
            cs = slice(slab * LANES, (slab + 1) * LANES)
            for r2 in range(4):
                mo_ref[slab, pl.ds(r2, n4, stride=4), :] = po_ref[r2, :, cs]
                ml_ref[slab, pl.ds(r2, n4, stride=4), :] = pl_ref[r2, :, cs]
    first = pl.program_id(2) == 0
    nj = tu // BLOCK
    lo = _lo_mask((BLOCK, LANES))
    qi = lax.broadcasted_iota(jnp.int32, (BLOCK, 2 * BLOCK), 0)
    kj = lax.broadcasted_iota(jnp.int32, (BLOCK, 2 * BLOCK), 1)
    dist = qi + BLOCK - kj
    valid = jnp.logical_and(dist >= 0, dist <= max_dist)
    valid_first = jnp.logical_and(valid, jnp.logical_or(kj >= BLOCK, jnp.logical_not(first)))
    distf = (dist * dil).astype(F32)

    n_heads = 2 * len(kcols)
    biases = []
    for head in range(n_heads):
        bias = -(slopes[head] * LOG2E) * distf
        biases.append((jnp.where(valid_first, bias, NEG), jnp.where(valid, bias, NEG)))

    def window(cur_ref, halo_ref, j, c0):
        if j == 0:
            return jnp.concatenate([halo_ref[:, c0:c0 + LANES], cur_ref[0:BLOCK, c0:c0 + LANES]], axis=0)
        return cur_ref[(j - 1) * BLOCK:(j + 1) * BLOCK, c0:c0 + LANES]

    def scores(hp, j, half):
        q2 = q_ref[j * BLOCK:(j + 1) * BLOCK, hp * LANES:(hp + 1) * LANES]
        qm = jnp.where(lo, q2, 0) if half == 0 else jnp.where(lo, 0, q2)
        kw = window(kc_ref, kp_ref, j, kcols[hp])
        return lax.dot_general(qm, kw, (((1,), (1,)), ((), ())), preferred_element_type=F32)

    halves = {}

    def finish(hp, j, half, s):
        s = s + biases[2 * hp + half][0 if j == 0 else 1]
        m = jnp.max(s, axis=-1, keepdims=True)
        if use_sink:
            sink = scal_ref[1 + 2 * hp + half]
            m = jnp.maximum(m, sink)
        p = jnp.exp2(s - m)
        l = jnp.sum(p, axis=-1, keepdims=True)
        if use_sink:
            l = l + jnp.exp2(sink - m)
        vw = window(vc_ref, vp_ref, j, vcols[hp])
        pv = jnp.dot(p.astype(BF16), vw, preferred_element_type=F32)
        halves[half] = (pv * (1.0 / l), (m + jnp.log2(l)) * (1.0 / LOG2E))
        if half == 0:
            return
        rows = slice(j * BLOCK, (j + 1) * BLOCK)
        cols = slice(hp * LANES, (hp + 1) * LANES)
        o_blk = jnp.where(lo, halves[0][0], halves[1][0])
        l_blk = jnp.where(lo, halves[0][1], halves[1][1])
        if merge_in:
            o_prev, l_prev = mo_ref[hp, rows, :], ml_ref[hp, rows, :]
            mx = jnp.maximum(l_blk, l_prev)
            w_own, w_prev = jnp.exp(l_blk - mx), jnp.exp(l_prev - mx)
            den = w_own + w_prev
            o_blk = (w_own * o_blk + w_prev * o_prev) * (1.0 / den)
            l_blk = mx + jnp.log(den)
        o_ref[rows, cols] = o_blk.astype(o_ref.dtype)
        if emit_lse:
            lse_ref[rows, cols] = l_blk

    items = [(hp, j, half) for hp in range(len(kcols)) for j in range(nj) for half in range(2)]
    pending = [scores(*it) for it in items[:lookahead]]
    for t, it in enumerate(items):
        if t + lookahead < len(items):
            pending.append(scores(*items[t + lookahead]))
        finish(*it, pending.pop(0))


def _banded_tables(slopes, dil, shift, qgroup):
    def split3(x):
        x = np.asarray(x, np.float32)
        hi = x.astype(jnp.bfloat16).astype(np.float32)
        mid = (x - hi).astype(jnp.bfloat16).astype(np.float32)
        return hi, mid, x - hi - mid

    wk = qgroup + BLOCK
    krow = np.arange(wk)
    k1, k2 = (krow & ~1).astype(np.float32), (krow & 1).astype(np.float32)
    ktab = np.zeros((2, wk, LANES), np.float32)
    qconst = np.zeros((len(slopes), qgroup, LANES), np.float32)
    qshift = np.zeros((len(slopes), qgroup, LANES), np.float32)
    for half in range(2):
        base = HEAD_DIM * (1 - half)
        ktab[half, :, base:base + 3] = k1[:, None]
        ktab[half, :, base + 3:base + 6] = k2[:, None]
        ktab[half, :, base + 6:base + 10] = 1.0
    for head, slope in enumerate(slopes):
        base = HEAD_DIM * (1 - head % 2)
        sigma = np.float32(slope) * np.float32(dil) * np.float32(LOG2E)
        s3 = split3(sigma)
        t3 = split3(-sigma * (np.arange(qgroup, dtype=np.float32) + BLOCK))
        for part in range(3):
            qconst[head, :, base + part] = s3[part]
            qconst[head, :, base + 3 + part] = s3[part]
            qconst[head, :, base + 6 + part] = t3[part]
        qshift[head, :, base + 9] = -1.0
    qtab = (jnp.asarray(qconst) + jnp.asarray(qshift) * shift).astype(BF16)
    return qtab, jnp.asarray(ktab).astype(BF16)


def _banded_fixed_kernel(*refs, tu, qgroup, max_dist, kcols, vcols, use_sink, merge_in, emit_lse,
                         lookahead=3):
    refs = list(refs)
    scal_ref, qtab_ref, ktab_ref, q_ref, kc_ref, kp_ref, vc_ref, vp_ref = refs[:8]
    refs = refs[8:]
    if merge_in:
        po_ref, pl_ref = refs[:2]
        refs = refs[2:]
    o_ref = refs.pop(0)
    lse_ref = refs.pop(0) if emit_lse else None
    if merge_in:
        mo_ref, ml_ref = refs
        n4 = tu // 4
        for slab in range(4):
            cs = slice(slab * LANES, (slab + 1) * LANES)
            for r2 in range(4):
                mo_ref[slab, pl.ds(r2, n4, stride=4), :] = po_ref[r2, :, cs]
                ml_ref[slab, pl.ds(r2, n4, stride=4), :] = pl_ref[r2, :, cs]
    first = pl.program_id(2) == 0
    shift = scal_ref[0]
    wk = qgroup + BLOCK
    ngroups = tu // qgroup
    krow = lax.broadcasted_iota(jnp.int32, (wk, qgroup), 0)
    qcol = lax.broadcasted_iota(jnp.int32, (wk, qgroup), 1)
    dist = qcol + BLOCK - krow
    valid = jnp.logical_and(dist >= 0, dist <= max_dist)
    valid_first = jnp.logical_and(valid, jnp.logical_or(krow >= BLOCK, jnp.logical_not(first)))
    lo_q = _lo_mask((qgroup, LANES))
    lo_k = _lo_mask((wk, LANES))
    ones_rows = (lax.broadcasted_iota(jnp.int32, (VT_ROWS - HEAD_DIM, wk), 0) == 0).astype(BF16)

    def window(cur_ref, halo_ref, g, c0):
        if g == 0:
            return jnp.concatenate([halo_ref[:, c0:c0 + LANES], cur_ref[0:qgroup, c0:c0 + LANES]], axis=0)
        return cur_ref[g * qgroup - BLOCK:(g + 1) * qgroup, c0:c0 + LANES]

    def scores(hp, g, half):
        q2 = q_ref[g * qgroup:(g + 1) * qgroup, hp * LANES:(hp + 1) * LANES]
        kw = window(kc_ref, kp_ref, g, kcols[hp])
        own_q = lo_q if half == 0 else jnp.logical_not(lo_q)
        own_k = lo_k if half == 0 else jnp.logical_not(lo_k)
        q_aug = jnp.where(own_q, q2, qtab_ref[2 * hp + half])
        k_aug = jnp.where(own_k, kw, ktab_ref[half])
        return lax.dot_general(k_aug, q_aug, (((1,), (1,)), ((), ())), preferred_element_type=F32)

    state = {}

    def finish(hp, g, half, s):
        s = jnp.where(valid_first if g == 0 else valid, s, NEG)
        p = jnp.exp2(s).astype(BF16)
        if half == 0:
            vw = window(vc_ref, vp_ref, g, vcols[hp])
            state["vt"] = vw.astype(F32).T
        vt = state["vt"][half * HEAD_DIM:(half + 1) * HEAD_DIM, :].astype(BF16)
        acc = jnp.dot(jnp.concatenate([vt, ones_rows], axis=0), p, preferred_element_type=F32)
        l = acc[HEAD_DIM:HEAD_DIM + 1, :]
        if use_sink:
            l = l + jnp.exp2(scal_ref[1 + 2 * hp + half] - shift)
        state[half] = (acc[0:HEAD_DIM, :] * (1.0 / l), l)
        if half == 0:
            return
        rows = slice(g * qgroup, (g + 1) * qgroup)
        cols = slice(hp * LANES, (hp + 1) * LANES)
        o_blk = jnp.concatenate([state[0][0], state[1][0]], axis=0).T
        if merge_in or emit_lse:
            lse_t = jnp.concatenate(
                [jnp.broadcast_to((jnp.log2(state[hf][1]) + shift) * (1.0 / LOG2E), (HEAD_DIM, qgroup))
                 for hf in range(2)], axis=0)
            l_blk = lse_t.T
        if merge_in:
            o_prev, l_prev = mo_ref[hp, rows, :], ml_ref[hp, rows, :]
            mx = jnp.maximum(l_blk, l_prev)
            w_own, w_prev = jnp.exp(l_blk - mx), jnp.exp(l_prev - mx)
            den = w_own + w_prev
            o_blk = (w_own * o_blk + w_prev * o_prev) * (1.0 / den)
            l_blk = mx + jnp.log(den)
        o_ref[rows, cols] = o_blk.astype(o_ref.dtype)
        if emit_lse:
            lse_ref[rows, cols] = l_blk

    items = [(hp, g, half) for hp in range(len(kcols)) for g in range(ngroups) for half in range(2)]
    pending = [scores(*it) for it in items[:lookahead]]
    for t, it in enumerate(items):
        if t + lookahead < len(items):
            pending.append(scores(*items[t + lookahead]))
        finish(*it, pending.pop(0))


def _banded(src, *, dil, groups, kcols, vcols, max_dist, slopes, shift, fixed, sinks=None, prev=None,
            emit_lse=False):
    bsz, _, u, width = src.shape
    tu = min(512, u)
    per = tu // BLOCK

    def spec(grp, halo):
        if halo:
            return pl.BlockSpec((None, None, BLOCK, 512),
                                lambda b, r, i: (b, r, jnp.maximum(i * per - 1, 0), grp))
        return pl.BlockSpec((None, None, tu, 512), lambda b, r, i: (b, r, i, grp))

    qg, kg, vg = groups
    in_specs = [spec(qg, False), spec(kg, False), spec(kg, True), spec(vg, False), spec(vg, True)]
    args = [src] * 5
    use_sink = sinks is not None
    scal = jnp.reshape(shift, (1,)).astype(F32)
    if use_sink:
        scal = jnp.concatenate([scal, sinks.astype(F32) * LOG2E])
    qgroup = min(2 * BLOCK, tu)
    if fixed:
        qtab, ktab = _banded_tables(slopes, dil, shift, qgroup)
        in_specs = [pl.BlockSpec(qtab.shape, lambda b, r, i: (0, 0, 0)),
                    pl.BlockSpec(ktab.shape, lambda b, r, i: (0, 0, 0))] + in_specs
        args = [qtab, ktab] + args
    in_specs = [pl.BlockSpec(memory_space=pltpu.SMEM)] + in_specs
    args = [scal] + args
    scratch = []
    if prev is not None:
        in_specs += [pl.BlockSpec((None, 4, tu // 4, 512), lambda b, r, i: (b, r, i, 0))] * 2
        args += list(prev)
        scratch = [pltpu.VMEM((4, tu, LANES), F32)] * 2
    out_spec = pl.BlockSpec((None, None, tu, 512), lambda b, r, i: (b, r, i, 0))
    if emit_lse:
        out_specs = [out_spec, out_spec]
        out_shape = [jax.ShapeDtypeStruct((bsz, dil, u, 512), F32)] * 2
    else:
        out_specs = out_spec
        out_shape = jax.ShapeDtypeStruct((bsz, dil, u, 512), BF16)
    common = dict(tu=tu, max_dist=max_dist, kcols=kcols, vcols=vcols, use_sink=use_sink,
                  merge_in=prev is not None, emit_lse=emit_lse)
    if fixed:
        kern = functools.partial(_banded_fixed_kernel, qgroup=qgroup, **common)
    else:
        kern = functools.partial(_banded_kernel, dil=dil, slopes=tuple(float(s) for s in slopes), **common)
    return pl.pallas_call(
        kern,
        grid=(bsz, dil, u // tu),
        in_specs=in_specs,
        out_specs=out_specs,
        out_shape=out_shape,
        scratch_shapes=scratch,
        compiler_params=_cparams(("parallel", "parallel", "arbitrary")),
        name=f"banded_d{dil}" + ("_sink" if use_sink else "") + ("" if fixed else "_online_max"),
    )(*args)


def _post_kernel(*refs, n_y, ff_chunk):
    y_refs = refs[:n_y]
    x_ref, wo_ref, g_ref, wu_ref, wd_ref, out_ref = refs[n_y:]
    y = y_refs[0][...] if n_y == 1 else jnp.concatenate([r[...] for r in y_refs], axis=1)
    x1 = x_ref[...] + jnp.dot(y, wo_ref[...], preferred_element_type=F32)
    h = _rms(x1, g_ref[...]).astype(BF16)
    acc = x1
    d_ff = wu_ref.shape[1]
    for f in range(d_ff // ff_chunk):
        cs = slice(f * ff_chunk, (f + 1) * ff_chunk)
        u = jnp.maximum(jnp.dot(h, wu_ref[:, cs], preferred_element_type=F32), 0.0)
        acc = acc + jnp.dot((u * u).astype(BF16), wd_ref[cs, :], preferred_element_type=F32)
    out_ref[...] = acc


def _post(ys, x2, w_out, g, w_up, w_down, tm):
    n, d = x2.shape
    d_ff = w_up.shape[1]

    def rows(width):
        return pl.BlockSpec((tm, width), lambda i: (i, 0))

    def whole(shape):
        return pl.BlockSpec(shape, lambda i: (0, 0), pipeline_mode=pl.Buffered(1))

    in_specs = [rows(y.shape[1]) for y in ys] + [
        rows(d), whole((d, d)), whole((1, d)), whole((d, d_ff)), whole((d_ff, d))]
    return pl.pallas_call(
        functools.partial(_post_kernel, n_y=len(ys), ff_chunk=1024),
        grid=(n // tm,),
        in_specs=in_specs,
        out_specs=rows(d),
        out_shape=jax.ShapeDtypeStruct((n, d), F32),
        compiler_params=_cparams(("parallel",)),
        name=f"post_{len(ys)}",
    )(*ys, x2, w_out, g, w_up, w_down)


def _split3(x):
    hi = x.astype(BF16).astype(F32)
    r = x - hi
    mid = r.astype(BF16).astype(F32)
    return hi, mid, r - mid


def _in_odd_kernel(x_ref, g_ref, w_ref, wvt_ref, eq_ref, ek_ref, gains_ref, bf_ref,
                   qa_ref, ka_ref, vt_ref, cfirst_ref, clast_ref, carry_ref, *, tiles_per_seq):
    i = pl.program_id(0)
    tm = x_ref.shape[0]
    rt = tm // ROW_GROUPS
    d2 = w_ref.shape[1] - LANES
    npair = d2 // (2 * LANES)

    @pl.when(i % tiles_per_seq == 0)
    def _():
        carry_ref[...] = jnp.zeros_like(carry_ref)

    r = lax.broadcasted_iota(jnp.int32, (rt, rt), 0)
    c = lax.broadcasted_iota(jnp.int32, (rt, rt), 1)
    tri = (c <= r).astype(F32)
    lane = lax.broadcasted_iota(jnp.int32, (rt, LANES), 1)
    lo = lane < HEAD_DIM
    ones_rows = (lax.broadcasted_iota(jnp.int32, (VT_ROWS - HEAD_DIM, rt), 0) == 0).astype(vt_ref.dtype)

    for grp in range(ROW_GROUPS):
        rows = slice(grp * rt, (grp + 1) * rt)
        h = _rms(x_ref[rows, :], g_ref[...]).astype(BF16)
        z = jnp.dot(h, w_ref[:, d2:], preferred_element_type=F32) + bf_ref[...]
        proj = jnp.dot(h, w_ref[:, :d2], preferred_element_type=F32)
        vt_all = lax.dot_general(wvt_ref[...], h, (((1,), (1,)), ((), ())),
                                 preferred_element_type=F32)
        log_f = jnp.minimum(z, 0.0) - jnp.log1p(jnp.exp(-jnp.abs(z)))
        local = jnp.dot(tri, log_f, preferred_element_type=F32, precision=lax.Precision.HIGHEST)
        csum = local + carry_ref[0:1, :]
        carry_ref[0:1, :] = csum[rt - 1:rt, :]
        c2 = csum * LOG2E
        if grp == 0:
            cfirst_ref[...] = c2[0:1, :]
        if grp == ROW_GROUPS - 1:
            clast_ref[...] = c2[rt - 1:rt, :]

        hi, mid, low = _split3(c2)
        a = jnp.where(lane < 16, hi, jnp.where(lane < 32, pltpu.roll(mid, 16, 1),
                      jnp.where(lane < 48, pltpu.roll(low, 32, 1), jnp.where(lane == 48, 1.0, 0.0))))
        a = a.astype(BF16)
        q_spare = jnp.dot(a, eq_ref[...], preferred_element_type=F32)
        k_spare = jnp.dot(a, ek_ref[...], preferred_element_type=F32)

        for hp in range(npair):
            qp = _pair_norm(proj[:, hp * LANES:(hp + 1) * LANES], gains_ref[0:1, :])
            kp = _pair_norm(proj[:, (npair + hp) * LANES:(npair + hp + 1) * LANES], gains_ref[1:2, :])
            qs = (qp, pltpu.roll(qp, HEAD_DIM, 1))
            ks = (kp, pltpu.roll(kp, HEAD_DIM, 1))
            for half in range(2):
                hd = 2 * hp + half
                cols = slice(hd * LANES, (hd + 1) * LANES)
                ka_ref[hd, rows, :] = jnp.where(lo, ks[half], k_spare[:, cols]).astype(ka_ref.dtype)
                qa_ref[hd, rows, :] = jnp.where(lo, qs[half], q_spare[:, cols]).astype(qa_ref.dtype)

        for hd in range(2 * npair):
            vt_ref[hd, 0:HEAD_DIM, rows] = vt_all[hd * HEAD_DIM:(hd + 1) * HEAD_DIM, :].astype(vt_ref.dtype)
            vt_ref[hd, HEAD_DIM:VT_ROWS, rows] = ones_rows


def _spare_lane_placement(n_heads, shift):
    eq = np.zeros((LANES, n_heads * LANES), np.float32)
    ek = np.zeros((LANES, n_heads * LANES), np.float32)
    sh = np.zeros((LANES, n_heads * LANES), np.float32)
    for h in range(n_heads):
        base = h * LANES + HEAD_DIM
        for part in range(3):
            ek[16 * part + h, base + part] = 1.0
            eq[16 * part + h, base + 3 + part] = 1.0
        eq[48, base:base + 3] = -1.0
        ek[48, base + 3:base + 7] = 1.0
        sh[48, base + 6] = -1.0
    return (jnp.asarray(eq) + jnp.asarray(sh) * shift).astype(BF16), jnp.asarray(ek).astype(BF16)


def _in_odd(x2, g, w_qkf, w_vt, eq, ek, gains, bf, seq, tm, n_heads):
    n, d = x2.shape
    tiles_per_seq = seq // tm
    bsz = n // seq

    def bt(i):
        return i // tiles_per_seq, i % tiles_per_seq

    def whole(a):
        return pl.BlockSpec(a.shape, lambda i: (0, 0))

    return pl.pallas_call(
        functools.partial(_in_odd_kernel, tiles_per_seq=tiles_per_seq),
        grid=(n // tm,),
        in_specs=[pl.BlockSpec((tm, d), lambda i: (i, 0)), whole(g), whole(w_qkf), whole(w_vt),
                  whole(eq), whole(ek), whole(gains), whole(bf)],
        out_specs=[
            pl.BlockSpec((None, n_heads, tm, LANES), lambda i: (bt(i)[0], 0, bt(i)[1], 0)),
            pl.BlockSpec((None, n_heads, tm, LANES), lambda i: (bt(i)[0], 0, bt(i)[1], 0)),
            pl.BlockSpec((None, n_heads, None, VT_ROWS, tm), lambda i: (bt(i)[0], 0, bt(i)[1], 0, 0)),
            pl.BlockSpec((None, 1, LANES), lambda i: (i, 0, 0)),
            pl.BlockSpec((None, 1, LANES), lambda i: (i, 0, 0)),
        ],
        out_shape=[
            jax.ShapeDtypeStruct((bsz, n_heads, seq, LANES), BF16),
            jax.ShapeDtypeStruct((bsz, n_heads, seq, LANES), BF16),
            jax.ShapeDtypeStruct((bsz, n_heads, tiles_per_seq, VT_ROWS, tm), BF16),
            jax.ShapeDtypeStruct((n // tm, 1, LANES), F32),
            jax.ShapeDtypeStruct((n // tm, 1, LANES), F32),
        ],
        scratch_shapes=[pltpu.VMEM((8, LANES), F32)],
        compiler_params=_cparams(("arbitrary",)),
        name="in_odd",
    )(x2, g, w_qkf, w_vt, eq, ek, gains, bf)


def _fox_kernel(cfirst_ref, clast_ref, q_ref, k_ref, vt_ref, o_ref, m_ref, acc_ref,
                *, tq, kt, sub, depth, online_max):
    assert kt % sub == 0 and tq % kt == 0
    b = pl.program_id(0)
    pair = pl.program_id(1)
    i = pl.program_id(2)
    nsub = tq // sub
    per = tq // kt
    m_ref[...] = jnp.full(m_ref.shape, NEG, F32)
    acc_ref[...] = jnp.zeros(acc_ref.shape, F32)
    krow = lax.broadcasted_iota(jnp.int32, (kt, sub), 0)
    qcol = lax.broadcasted_iota(jnp.int32, (kt, sub), 1)

    def scores(h, j, kb):
        qs = q_ref[h, j * sub:(j + 1) * sub, :]
        return lax.dot_general(kb, qs, (((1,), (1,)), ((), ())), preferred_element_type=F32)

    def update(h, j, s, vt, diag):
        idx = h * nsub + j
        if diag is not None:
            s = jnp.where(krow + diag <= qcol, s, NEG)
        if online_max:
            m_old = m_ref[idx]
            m_new = jnp.maximum(m_old, jnp.max(s, axis=0, keepdims=True))
            alpha = jnp.exp2(m_old - m_new)
            p = jnp.exp2(s - m_new).astype(BF16)
            m_ref[idx] = m_new
            acc_ref[idx] = alpha * acc_ref[idx] + jnp.dot(vt, p, preferred_element_type=F32)
        else:
            acc_ref[idx] += jnp.dot(vt, jnp.exp2(s).astype(BF16), preferred_element_type=F32)

    def run(work):
        pending = [scores(*w[:3]) for w in work[:depth]]
        for t, (h, j, kb, vt, diag) in enumerate(work):
            if t + depth < len(work):
                pending.append(scores(*work[t + depth][:3]))
            update(h, j, pending.pop(0), vt, diag)

    def key_block(h, blk):
        return k_ref[h, pl.ds(pl.multiple_of(blk * kt, kt), kt), :]

    def body(kv, carry):
        run([(h, j, key_block(h, kv), vt_ref[h, kv], None) for h in range(2) for j in range(nsub)])
        return carry

    first = 0
    if not online_max:
        def negligible(kv):
            gaps = [cfirst_ref[b, 2 * pair + h, i] - clast_ref[b, 2 * pair + h, kv] for h in range(2)]
            return jnp.maximum(gaps[0], gaps[1]) < -SKIP_LOG2

        first = lax.while_loop(lambda kv: jnp.logical_and(kv < i * per, negligible(kv)),
                               lambda kv: kv + 1, 0)
    lax.fori_loop(first, i * per, body, 0)

    work = []
    for h in range(2):
        for jk in range(per):
            kb = key_block(h, i * per + jk)
            vt = vt_ref[h, i * per + jk]
            for j in range(nsub):
                diag = jk * kt - j * sub
                if diag >= sub:
                    continue
                work.append((h, j, kb, vt, diag if diag + kt - 1 > 0 else None))
    run(work)

    for j in range(nsub):
        outs = []
        for h in range(2):
            acc = acc_ref[h * nsub + j]
            outs.append(acc[0:HEAD_DIM, :] * (1.0 / acc[HEAD_DIM:HEAD_DIM + 1, :]))
        pair_t = jnp.concatenate(outs, axis=0)
        o_ref[j * sub:(j + 1) * sub, :] = pair_t.T.astype(o_ref.dtype)


def _fox(cfirst, clast, q_aug, k_aug, vt, tq, sub, online_max):
    bsz, n_heads, seq, _ = q_aug.shape
    nkt, kt = vt.shape[2], vt.shape[4]
    npair = n_heads // 2
    nsub = tq // sub
    return pl.pallas_call(
        functools.partial(_fox_kernel, tq=tq, kt=kt, sub=sub, depth=4, online_max=online_max),
        grid=(bsz, npair, seq // tq),
        in_specs=[
            pl.BlockSpec(memory_space=pltpu.SMEM),
            pl.BlockSpec(memory_space=pltpu.SMEM),
            pl.BlockSpec((None, 2, tq, LANES), lambda b, p, i: (b, p, i, 0)),
            pl.BlockSpec((None, 2, seq, LANES), lambda b, p, i: (b, p, 0, 0)),
            pl.BlockSpec((None, 2, nkt, VT_ROWS, kt), lambda b, p, i: (b, p, 0, 0, 0)),
        ],
        out_specs=pl.BlockSpec((None, tq, LANES), lambda b, p, i: (b, i, p)),
        out_shape=jax.ShapeDtypeStruct((bsz, seq, n_heads * HEAD_DIM), BF16),
        scratch_shapes=[pltpu.VMEM((2 * nsub, 1, sub), F32), pltpu.VMEM((2 * nsub, VT_ROWS, sub), F32)],
        compiler_params=_cparams(("parallel", "parallel", "arbitrary")),
        name="fox_online_max" if online_max else "fox",
    )(cfirst, clast, q_aug, k_aug, vt).reshape(bsz * seq, n_heads * HEAD_DIM)


def _pair_gain(gain, scale=1.0):
    return jnp.tile(gain.astype(F32) * scale, 2)


def _softmax_shift(q_gain, k_gain):
    bound = (HEAD_DIM ** 0.5 * LOG2E) * jnp.max(jnp.abs(q_gain)) * jnp.max(jnp.abs(k_gain))
    return (bound.astype(F32) * 1.02).astype(BF16).astype(F32)


def _forward(x, g_mix, g_mlp, w_in_even, a_q_gain, a_k_gain, b_q_gain, b_k_gain, b_sinks,
             w_out_even, w_in_odd, b_forget, c_q_gain, c_k_gain, w_out_odd, w_up, w_down):
    bsz, seq, d = x.shape
    n = bsz * seq
    depth = g_mix.shape[0]
    scale = HEAD_DIM ** -0.5
    n_heads_a = d // (2 * HEAD_DIM)
    n_heads_b = d // (2 * HEAD_DIM)
    n_heads_c = d // HEAD_DIM
    slopes = _alibi_slopes(n_heads_a + n_heads_b)
    x2 = x.reshape(n, d)
    tm = 512

    for layer in range(depth):
        i = layer // 2
        g1 = g_mix[layer].reshape(1, d)
        g2 = g_mlp[layer].reshape(1, d)
        wu = w_up[layer].astype(BF16)
        wd = w_down[layer].astype(BF16)
        if layer % 2 == 0:
            gains = jnp.stack([_pair_gain(a_q_gain[i], scale * LOG2E), _pair_gain(a_k_gain[i]),
                               _pair_gain(b_q_gain[i], scale * LOG2E), _pair_gain(b_k_gain[i])])
            nat, by4, by16 = _in_even(x2, g1, w_in_even[i].astype(BF16), gains, seq, tm)
            nat = nat.reshape(bsz, 1, seq, nat.shape[1])
            shift_a = _softmax_shift(a_q_gain[i], a_k_gain[i])
            shift_b = _softmax_shift(b_q_gain[i], b_k_gain[i])

            def attention(nat, by4, by16, sinks, fixed):
                dilated = dict(groups=(0, 1, 2), kcols=(0, 128, 256, 384), vcols=(0, 128, 256, 384),
                               max_dist=BLOCK, slopes=slopes[n_heads_b:], shift=shift_a, fixed=fixed)
                part = _banded(by16, dil=16, emit_lse=True, **dilated)
                part = _banded(by4, dil=4, prev=part, emit_lse=True, **dilated)
                a_out = _banded(nat, dil=1, prev=part, **dilated)
                b_out = _banded(nat, dil=1, groups=(3, 4, 4), kcols=(0, 0, 128, 128),
                                vcols=(256, 256, 384, 384), max_dist=WINDOW_B - 1, slopes=slopes[:n_heads_b],
                                shift=shift_b, fixed=fixed, sinks=sinks)
                return a_out.reshape(n, 512), b_out.reshape(n, 512)

            ys = lax.cond(jnp.maximum(shift_a, shift_b) <= MAX_FIXED_SHIFT,
                          lambda *a: attention(*a, fixed=True),
                          lambda *a: attention(*a, fixed=False),
                          nat, by4, by16, b_sinks[i].astype(F32))
            x2 = _post(list(ys), x2, w_out_even[i].astype(BF16), g2, wu, wd, 256)
        else:
            w = w_in_odd[i]
            w_qkf = jnp.pad(jnp.concatenate([w[:, :2 * d], w[:, 3 * d:]], axis=1),
                            ((0, 0), (0, LANES - n_heads_c))).astype(BF16)
            w_vt = w[:, 2 * d:3 * d].T.astype(BF16)
            shift = _softmax_shift(c_q_gain[i], c_k_gain[i])
            gains = jnp.stack([_pair_gain(c_q_gain[i], scale * LOG2E), _pair_gain(c_k_gain[i])])
            bf = jnp.pad(b_forget[i].astype(F32), (0, LANES - n_heads_c)).reshape(1, LANES)
            eq, ek = _spare_lane_placement(n_heads_c, shift)
            q_aug, k_aug, vt, cfirst, clast = _in_odd(x2, g1, w_qkf, w_vt, eq, ek, gains, bf, seq, tm, n_heads_c)
            tq = min(FOX_TQ, seq)
            cfirst = cfirst.reshape(bsz, seq // tm, LANES)[:, ::tq // tm, :n_heads_c].transpose(0, 2, 1)
            clast = clast.reshape(bsz, seq // tm, LANES)[:, :, :n_heads_c].transpose(0, 2, 1)
            fox = functools.partial(_fox, tq=tq, sub=256)
            y = lax.cond(shift <= MAX_FIXED_SHIFT,
                         lambda *a: fox(*a, online_max=False),
                         lambda *a: fox(*a, online_max=True),
                         cfirst, clast, q_aug, k_aug, vt)
            x2 = _post([y], x2, w_out_odd[i].astype(BF16), g2, wu, wd, 256)
    return x2.reshape(bsz, seq, d)


def kernel(x, g_mix, g_mlp, w_in_even, a_q_gain, a_k_gain, b_q_gain, b_k_gain, b_sinks, w_out_even,
           w_in_odd, b_forget, c_q_gain, c_k_gain, w_out_odd, w_up, w_down):
    return _forward(x, g_mix, g_mlp, w_in_even, a_q_gain, a_k_gain, b_q_gain, b_k_gain, b_sinks,
                    w_out_even, w_in_odd, b_forget, c_q_gain, c_k_gain, w_out_odd, w_up, w_down)
```

```python
import functools

import numpy as np
import jax
import jax.numpy as jnp
from jax import lax
from jax.experimental import pallas as pl
from jax.experimental.pallas import tpu as pltpu

F32 = jnp.float32
BF16 = jnp.bfloat16

HEAD_DIM = 64
LANES = 128
EPS = 1e-6
NEG = -1e30
BLOCK = 128
DILATIONS = (1, 4, 16)
WINDOW_B = 128
LOG2E = float(np.log2(np.e))
VT_ROWS = 80
ROW_GROUPS = 2
FOX_TQ = 2048
SKIP_LOG2 = 160.0
MAX_FIXED_SHIFT = 40.0
VMEM_LIMIT = 56 * 1024 * 1024


def _alibi_slopes(n):
    return np.asarray(2.0 ** (-8.0 * np.arange(1, n + 1) / n), dtype=np.float32)


def _cparams(sem):
    return pltpu.CompilerParams(dimension_semantics=sem, vmem_limit_bytes=VMEM_LIMIT)


def _rms(x, gain):
    return x * lax.rsqrt(jnp.mean(x * x, axis=-1, keepdims=True) + EPS) * gain


def _lo_mask(shape):
    return lax.broadcasted_iota(jnp.int32, shape, len(shape) - 1) < HEAD_DIM


def _pair_norm(xc, gain2):
    lo = _lo_mask(xc.shape)
    sq = xc * xc
    s_lo = jnp.sum(jnp.where(lo, sq, 0.0), axis=-1, keepdims=True)
    s_hi = jnp.sum(jnp.where(lo, 0.0, sq), axis=-1, keepdims=True)
    r = jnp.where(lo, lax.rsqrt(s_lo / HEAD_DIM + EPS), lax.rsqrt(s_hi / HEAD_DIM + EPS))
    return xc * r * gain2


def _in_even_kernel(x_ref, g_ref, w_ref, gains_ref, o_ref, o4_ref, o16_ref, s1_ref, s2_ref):
    tm = x_ref.shape[0]
    rt = tm // ROW_GROUPS
    for grp in range(ROW_GROUPS):
        _in_even_rows(grp, rt, x_ref, g_ref, w_ref, gains_ref, o_ref, o4_ref, o16_ref, s1_ref, s2_ref)


def _in_even_rows(grp, rt, x_ref, g_ref, w_ref, gains_ref, o_ref, o4_ref, o16_ref, s1_ref, s2_ref):
    rows = slice(grp * rt, (grp + 1) * rt)
    n4, n16 = rt // 4, rt // 16
    h = _rms(x_ref[rows, :], g_ref[...]).astype(BF16)
    proj = jnp.dot(h, w_ref[...], preferred_element_type=F32)
    lo = _lo_mask((rt, LANES))

    def col(c):
        return proj[:, c * LANES:(c + 1) * LANES]

    def put(c, val):
        o_ref[rows, c * LANES:(c + 1) * LANES] = val.astype(o_ref.dtype)
        if c >= 12:
            return
        cols = slice(c * LANES, (c + 1) * LANES)
        s1_ref[grp, c] = val
        for r in range(4):
            part = s1_ref[grp, c, pl.ds(r, n4, stride=4), :]
            o4_ref[r, grp * n4:(grp + 1) * n4, cols] = part.astype(o4_ref.dtype)
            s2_ref[grp, c, r * n4:(r + 1) * n4, :] = part
        for r in range(4):
            for r2 in range(4):
                part = s2_ref[grp, c, pl.ds(r * n4 + r2, n16, stride=4), :]
                o16_ref[4 * r + r2, grp * n16:(grp + 1) * n16, cols] = part.astype(o16_ref.dtype)

    for c in range(4):
        put(c, _pair_norm(col(c), gains_ref[0:1, :]))
    for c in range(4, 8):
        put(c, _pair_norm(col(c), gains_ref[1:2, :]))
    for c in range(8, 12):
        put(c, col(c))
    for c in range(12, 16):
        put(c, _pair_norm(col(c), gains_ref[2:3, :]))
    bk = _pair_norm(col(16), gains_ref[3:4, :])
    bkr = pltpu.roll(bk, HEAD_DIM, 1)
    put(16, jnp.where(lo, bk, bkr))
    put(17, jnp.where(lo, bkr, bk))
    bv = col(17)
    bvr = pltpu.roll(bv, HEAD_DIM, 1)
    put(18, jnp.where(lo, bv, bvr))
    put(19, jnp.where(lo, bvr, bv))


def _in_even(x2, g, w, gains, seq, tm):
    n, d = x2.shape
    e = w.shape[1]
    bsz = n // seq
    tiles_per_seq = seq // tm
    a_width = 12 * LANES

    def regrouped(dil):
        return pl.BlockSpec((None, dil, tm // dil, a_width),
                            lambda i: (i // tiles_per_seq, 0, i % tiles_per_seq, 0))

    return pl.pallas_call(
        _in_even_kernel,
        grid=(n // tm,),
        in_specs=[
            pl.BlockSpec((tm, d), lambda i: (i, 0)),
            pl.BlockSpec((1, d), lambda i: (0, 0)),
            pl.BlockSpec((d, e), lambda i: (0, 0)),
            pl.BlockSpec((4, LANES), lambda i: (0, 0)),
        ],
        out_specs=[pl.BlockSpec((tm, 20 * LANES), lambda i: (i, 0)), regrouped(4), regrouped(16)],
        out_shape=[jax.ShapeDtypeStruct((n, 20 * LANES), BF16),
                   jax.ShapeDtypeStruct((bsz, 4, seq // 4, a_width), BF16),
                   jax.ShapeDtypeStruct((bsz, 16, seq // 16, a_width), BF16)],
        scratch_shapes=[pltpu.VMEM((ROW_GROUPS, 12, tm // ROW_GROUPS, LANES), F32)] * 2,
        compiler_params=_cparams(("parallel",)),
        name="in_even",
    )(x2, g, w, gains)


def _banded_kernel(*refs, tu, dil, max_dist, slopes, kcols, vcols, use_sink, merge_in, emit_lse,
                   lookahead=4):
    refs = list(refs)
    scal_ref, q_ref, kc_ref, kp_ref, vc_ref, vp_ref = refs[:6]
    refs = refs[6:]
    if merge_in:
        po_ref, pl_ref = refs[:2]
        refs = refs[2:]
    o_ref = refs.pop(0)
    lse_ref = refs.pop(0) if emit_lse else None
    if merge_in:
        mo_ref, ml_ref = refs
        n4 = tu // 4
        for slab in range(4):
            cs = slice(slab * LANES, (slab + 1) * LANES)
            for r2 in range(4):
                mo_ref[slab, pl.ds(r2, n4, stride=4), :] = po_ref[r2, :, cs]
                ml_ref[slab, pl.ds(r2, n4, stride=4), :] = pl_ref[r2, :, cs]
    first = pl.program_id(2) == 0
    nj = tu // BLOCK
    lo = _lo_mask((BLOCK, LANES))
    qi = lax.broadcasted_iota(jnp.int32, (BLOCK, 2 * BLOCK), 0)
    kj = lax.broadcasted_iota(jnp.int32, (BLOCK, 2 * BLOCK), 1)
    dist = qi + BLOCK - kj
    valid = jnp.logical_and(dist >= 0, dist <= max_dist)
    valid_first = jnp.logical_and(valid, jnp.logical_or(kj >= BLOCK, jnp.logical_not(first)))
    distf = (dist * dil).astype(F32)

    n_heads = 2 * len(kcols)
    biases = []
    for head in range(n_heads):
        bias = -(slopes[head] * LOG2E) * distf
        biases.append((jnp.where(valid_first, bias, NEG), jnp.where(valid, bias, NEG)))

    def window(cur_ref, halo_ref, j, c0):
        if j == 0:
            return jnp.concatenate([halo_ref[:, c0:c0 + LANES], cur_ref[0:BLOCK, c0:c0 + LANES]], axis=0)
        return cur_ref[(j - 1) * BLOCK:(j + 1) * BLOCK, c0:c0 + LANES]

    def scores(hp, j, half):
        q2 = q_ref[j * BLOCK:(j + 1) * BLOCK, hp * LANES:(hp + 1) * LANES]
        qm = jnp.where(lo, q2, 0) if half == 0 else jnp.where(lo, 0, q2)
        kw = window(kc_ref, kp_ref, j, kcols[hp])
        return lax.dot_general(qm, kw, (((1,), (1,)), ((), ())), preferred_element_type=F32)

    halves = {}

    def finish(hp, j, half, s):
        s = s + biases[2 * hp + half][0 if j == 0 else 1]
        m = jnp.max(s, axis=-1, keepdims=True)
        if use_sink:
            sink = scal_ref[1 + 2 * hp + half]
            m = jnp.maximum(m, sink)
        p = jnp.exp2(s - m)
        l = jnp.sum(p, axis=-1, keepdims=True)
        if use_sink:
            l = l + jnp.exp2(sink - m)
        vw = window(vc_ref, vp_ref, j, vcols[hp])
        pv = jnp.dot(p.astype(BF16), vw, preferred_element_type=F32)
        halves[half] = (pv * (1.0 / l), (m + jnp.log2(l)) * (1.0 / LOG2E))
        if half == 0:
            return
        rows = slice(j * BLOCK, (j + 1) * BLOCK)
        cols = slice(hp * LANES, (hp + 1) * LANES)
        o_blk = jnp.where(lo, halves[0][0], halves[1][0])
        l_blk = jnp.where(lo, halves[0][1], halves[1][1])
        if merge_in:
            o_prev, l_prev = mo_ref[hp, rows, :], ml_ref[hp, rows, :]
            mx = jnp.maximum(l_blk, l_prev)
            w_own, w_prev = jnp.exp(l_blk - mx), jnp.exp(l_prev - mx)
            den = w_own + w_prev
            o_blk = (w_own * o_blk + w_prev * o_prev) * (1.0 / den)
            l_blk = mx + jnp.log(den)
        o_ref[rows, cols] = o_blk.astype(o_ref.dtype)
        if emit_lse:
            lse_ref[rows, cols] = l_blk

    items = [(hp, j, half) for hp in range(len(kcols)) for j in range(nj) for half in range(2)]
    pending = [scores(*it) for it in items[:lookahead]]
    for t, it in enumerate(items):
        if t + lookahead < len(items):
            pending.append(scores(*items[t + lookahead]))
        finish(*it, pending.pop(0))


def _banded_tables(slopes, dil, shift, qgroup):
    def split3(x):
        x = np.asarray(x, np.float32)
        hi = x.astype(jnp.bfloat16).astype(np.float32)
        mid = (x - hi).astype(jnp.bfloat16).astype(np.float32)
        return hi, mid, x - hi - mid

    wk = qgroup + BLOCK
    krow = np.arange(wk)
    k1, k2 = (krow & ~1).astype(np.float32), (krow & 1).astype(np.float32)
    ktab = np.zeros((2, wk, LANES), np.float32)
    qconst = np.zeros((len(slopes), qgroup, LANES), np.float32)
    qshift = np.zeros((len(slopes), qgroup, LANES), np.float32)
    for half in range(2):
        base = HEAD_DIM * (1 - half)
        ktab[half, :, base:base + 3] = k1[:, None]
        ktab[half, :, base + 3:base + 6] = k2[:, None]
        ktab[half, :, base + 6:base + 10] = 1.0
    for head, slope in enumerate(slopes):
        base = HEAD_DIM * (1 - head % 2)
        sigma = np.float32(slope) * np.float32(dil) * np.float32(LOG2E)
        s3 = split3(sigma)
        t3 = split3(-sigma * (np.arange(qgroup, dtype=np.float32) + BLOCK))
        for part in range(3):
            qconst[head, :, base + part] = s3[part]
            qconst[head, :, base + 3 + part] = s3[part]
            qconst[head, :, base + 6 + part] = t3[part]
        qshift[head, :, base + 9] = -1.0
    qtab = (jnp.asarray(qconst) + jnp.asarray(qshift) * shift).astype(BF16)
    return qtab, jnp.asarray(ktab).astype(BF16)


def _banded_fixed_kernel(*refs, tu, qgroup, max_dist, kcols, vcols, use_sink, merge_in, emit_lse,
                         lookahead=3):
    refs = list(refs)
    scal_ref, qtab_ref, ktab_ref, q_ref, kc_ref, kp_ref, vc_ref, vp_ref = refs[:8]
    refs = refs[8:]
    if merge_in:
        po_ref, pl_ref = refs[:2]
        refs = refs[2:]
    o_ref = refs.pop(0)
    lse_ref = refs.pop(0) if emit_lse else None
    if merge_in:
        mo_ref, ml_ref = refs
        n4 = tu // 4
        for slab in range(4):
            cs = slice(slab * LANES, (slab + 1) * LANES)
            for r2 in range(4):
                mo_ref[slab, pl.ds(r2, n4, stride=4), :] = po_ref[r2, :, cs]
                ml_ref[slab, pl.ds(r2, n4, stride=4), :] = pl_ref[r2, :, cs]
    first = pl.program_id(2) == 0
    shift = scal_ref[0]
    wk = qgroup + BLOCK
    ngroups = tu // qgroup
    krow = lax.broadcasted_iota(jnp.int32, (wk, qgroup), 0)
    qcol = lax.broadcasted_iota(jnp.int32, (wk, qgroup), 1)
    dist = qcol + BLOCK - krow
    valid = jnp.logical_and(dist >= 0, dist <= max_dist)
    valid_first = jnp.logical_and(valid, jnp.logical_or(krow >= BLOCK, jnp.logical_not(first)))
    lo_q = _lo_mask((qgroup, LANES))
    lo_k = _lo_mask((wk, LANES))
    ones_rows = (lax.broadcasted_iota(jnp.int32, (VT_ROWS - HEAD_DIM, wk), 0) == 0).astype(BF16)

    def window(cur_ref, halo_ref, g, c0):
        if g == 0:
            return jnp.concatenate([halo_ref[:, c0:c0 + LANES], cur_ref[0:qgroup, c0:c0 + LANES]], axis=0)
        return cur_ref[g * qgroup - BLOCK:(g + 1) * qgroup, c0:c0 + LANES]

    def scores(hp, g, half):
        q2 = q_ref[g * qgroup:(g + 1) * qgroup, hp * LANES:(hp + 1) * LANES]
        kw = window(kc_ref, kp_ref, g, kcols[hp])
        own_q = lo_q if half == 0 else jnp.logical_not(lo_q)
        own_k = lo_k if half == 0 else jnp.logical_not(lo_k)
        q_aug = jnp.where(own_q, q2, qtab_ref[2 * hp + half])
        k_aug = jnp.where(own_k, kw, ktab_ref[half])
        return lax.dot_general(k_aug, q_aug, (((1,), (1,)), ((), ())), preferred_element_type=F32)

    state = {}

    def finish(hp, g, half, s):
        s = jnp.where(valid_first if g == 0 else valid, s, NEG)
        p = jnp.exp2(s).astype(BF16)
        if half == 0:
            vw = window(vc_ref, vp_ref, g, vcols[hp])
            state["vt"] = vw.astype(F32).T
        vt = state["vt"][half * HEAD_DIM:(half + 1) * HEAD_DIM, :].astype(BF16)
        acc = jnp.dot(jnp.concatenate([vt, ones_rows], axis=0), p, preferred_element_type=F32)
        l = acc[HEAD_DIM:HEAD_DIM + 1, :]
        if use_sink:
            l = l + jnp.exp2(scal_ref[1 + 2 * hp + half] - shift)
        state[half] = (acc[0:HEAD_DIM, :] * (1.0 / l), l)
        if half == 0:
            return
        rows = slice(g * qgroup, (g + 1) * qgroup)
        cols = slice(hp * LANES, (hp + 1) * LANES)
        o_blk = jnp.concatenate([state[0][0], state[1][0]], axis=0).T
        if merge_in or emit_lse:
            lse_t = jnp.concatenate(
                [jnp.broadcast_to((jnp.log2(state[hf][1]) + shift) * (1.0 / LOG2E), (HEAD_DIM, qgroup))
                 for hf in range(2)], axis=0)
            l_blk = lse_t.T
        if merge_in:
            o_prev, l_prev = mo_ref[hp, rows, :], ml_ref[hp, rows, :]
            mx = jnp.maximum(l_blk, l_prev)
            w_own, w_prev = jnp.exp(l_blk - mx), jnp.exp(l_prev - mx)
            den = w_own + w_prev
            o_blk = (w_own * o_blk + w_prev * o_prev) * (1.0 / den)
            l_blk = mx + jnp.log(den)
        o_ref[rows, cols] = o_blk.astype(o_ref.dtype)
        if emit_lse:
            lse_ref[rows, cols] = l_blk

    items = [(hp, g, half) for hp in range(len(kcols)) for g in range(ngroups) for half in range(2)]
    pending = [scores(*it) for it in items[:lookahead]]
    for t, it in enumerate(items):
        if t + lookahead < len(items):
            pending.append(scores(*items[t + lookahead]))
        finish(*it, pending.pop(0))


def _banded(src, *, dil, groups, kcols, vcols, max_dist, slopes, shift, fixed, sinks=None, prev=None,
            emit_lse=False):
    bsz, _, u, width = src.shape
    tu = min(512, u)
    per = tu // BLOCK

    def spec(grp, halo):
        if halo:
            return pl.BlockSpec((None, None, BLOCK, 512),
                                lambda b, r, i: (b, r, jnp.maximum(i * per - 1, 0), grp))
        return pl.BlockSpec((None, None, tu, 512), lambda b, r, i: (b, r, i, grp))

    qg, kg, vg = groups
    in_specs = [spec(qg, False), spec(kg, False), spec(kg, True), spec(vg, False), spec(vg, True)]
    args = [src] * 5
    use_sink = sinks is not None
    scal = jnp.reshape(shift, (1,)).astype(F32)
    if use_sink:
        scal = jnp.concatenate([scal, sinks.astype(F32) * LOG2E])
    qgroup = min(2 * BLOCK, tu)
    if fixed:
        qtab, ktab = _banded_tables(slopes, dil, shift, qgroup)
        in_specs = [pl.BlockSpec(qtab.shape, lambda b, r, i: (0, 0, 0)),
                    pl.BlockSpec(ktab.shape, lambda b, r, i: (0, 0, 0))] + in_specs
        args = [qtab, ktab] + args
    in_specs = [pl.BlockSpec(memory_space=pltpu.SMEM)] + in_specs
    args = [scal] + args
    scratch = []
    if prev is not None:
        in_specs += [pl.BlockSpec((None, 4, tu // 4, 512), lambda b, r, i: (b, r, i, 0))] * 2
        args += list(prev)
        scratch = [pltpu.VMEM((4, tu, LANES), F32)] * 2
    out_spec = pl.BlockSpec((None, None, tu, 512), lambda b, r, i: (b, r, i, 0))
    if emit_lse:
        out_specs = [out_spec, out_spec]
        out_shape = [jax.ShapeDtypeStruct((bsz, dil, u, 512), F32)] * 2
    else:
        out_specs = out_spec
        out_shape = jax.ShapeDtypeStruct((bsz, dil, u, 512), BF16)
    common = dict(tu=tu, max_dist=max_dist, kcols=kcols, vcols=vcols, use_sink=use_sink,
                  merge_in=prev is not None, emit_lse=emit_lse)
    if fixed:
        kern = functools.partial(_banded_fixed_kernel, qgroup=qgroup, **common)
    else:
        kern = functools.partial(_banded_kernel, dil=dil, slopes=tuple(float(s) for s in slopes), **common)
    return pl.pallas_call(
        kern,
        grid=(bsz, dil, u // tu),
        in_specs=in_specs,
        out_specs=out_specs,
        out_shape=out_shape,
        scratch_shapes=scratch,
        compiler_params=_cparams(("parallel", "parallel", "arbitrary")),
        name=f"banded_d{dil}" + ("_sink" if use_sink else "") + ("" if fixed else "_online_max"),
    )(*args)


def _post_kernel(*refs, n_y, ff_chunk):
    y_refs = refs[:n_y]
    x_ref, wo_ref, g_ref, wu_ref, wd_ref, out_ref = refs[n_y:]
    rt = x_ref.shape[0] // ROW_GROUPS
    groups = [slice(grp * rt, (grp + 1) * rt) for grp in range(ROW_GROUPS)]
    x1s = []
    for rows in groups:
        y = y_refs[0][rows, :] if n_y == 1 else jnp.concatenate([r[rows, :] for r in y_refs], axis=1)
        x1s.append(x_ref[rows, :] + jnp.dot(y, wo_ref[...], preferred_element_type=F32))
    d_ff = wu_ref.shape[1]
    for rows, x1 in zip(groups, x1s):
        h = _rms(x1, g_ref[...]).astype(BF16)
        acc = x1
        for f in range(d_ff // ff_chunk):
            cs = slice(f * ff_chunk, (f + 1) * ff_chunk)
            u = jnp.maximum(jnp.dot(h, wu_ref[:, cs], preferred_element_type=F32), 0.0)
            acc = acc + jnp.dot((u * u).astype(BF16), wd_ref[cs, :], preferred_element_type=F32)
        out_ref[rows, :] = acc


def _post(ys, x2, w_out, g, w_up, w_down, tm):
    n, d = x2.shape
    d_ff = w_up.shape[1]

    def rows(width):
        return pl.BlockSpec((tm, width), lambda i: (i, 0))

    def whole(shape):
        return pl.BlockSpec(shape, lambda i: (0, 0), pipeline_mode=pl.Buffered(1))

    in_specs = [rows(y.shape[1]) for y in ys] + [
        rows(d), whole((d, d)), whole((1, d)), whole((d, d_ff)), whole((d_ff, d))]
    return pl.pallas_call(
        functools.partial(_post_kernel, n_y=len(ys), ff_chunk=1024),
        grid=(n // tm,),
        in_specs=in_specs,
        out_specs=rows(d),
        out_shape=jax.ShapeDtypeStruct((n, d), F32),
        compiler_params=_cparams(("parallel",)),
        name=f"post_{len(ys)}",
    )(*ys, x2, w_out, g, w_up, w_down)


def _split3(x):
    hi = x.astype(BF16).astype(F32)
    r = x - hi
    mid = r.astype(BF16).astype(F32)
    return hi, mid, r - mid


def _in_odd_kernel(x_ref, g_ref, w_ref, wvt_ref, eq_ref, ek_ref, gains_ref, bf_ref,
                   qa_ref, ka_ref, vt_ref, cfirst_ref, clast_ref, carry_ref, *, tiles_per_seq):
    i = pl.program_id(0)
    tm = x_ref.shape[0]
    rt = tm // ROW_GROUPS
    d2 = w_ref.shape[1] - LANES
    npair = d2 // (2 * LANES)

    @pl.when(i % tiles_per_seq == 0)
    def _():
        carry_ref[...] = jnp.zeros_like(carry_ref)

    r = lax.broadcasted_iota(jnp.int32, (rt, rt), 0)
    c = lax.broadcasted_iota(jnp.int32, (rt, rt), 1)
    tri = (c <= r).astype(F32)
    lane = lax.broadcasted_iota(jnp.int32, (rt, LANES), 1)
    lo = lane < HEAD_DIM
    ones_rows = (lax.broadcasted_iota(jnp.int32, (VT_ROWS - HEAD_DIM, rt), 0) == 0).astype(vt_ref.dtype)

    for grp in range(ROW_GROUPS):
        rows = slice(grp * rt, (grp + 1) * rt)
        h = _rms(x_ref[rows, :], g_ref[...]).astype(BF16)
        z = jnp.dot(h, w_ref[:, d2:], preferred_element_type=F32) + bf_ref[...]
        proj = jnp.dot(h, w_ref[:, :d2], preferred_element_type=F32)
        vt_all = lax.dot_general(wvt_ref[...], h, (((1,), (1,)), ((), ())),
                                 preferred_element_type=F32)
        log_f = jnp.minimum(z, 0.0) - jnp.log1p(jnp.exp(-jnp.abs(z)))
        local = jnp.dot(tri, log_f, preferred_element_type=F32, precision=lax.Precision.HIGHEST)
        csum = local + carry_ref[0:1, :]
        carry_ref[0:1, :] = csum[rt - 1:rt, :]
        c2 = csum * LOG2E
        if grp == 0:
            cfirst_ref[...] = c2[0:1, :]
        if grp == ROW_GROUPS - 1:
            clast_ref[...] = c2[rt - 1:rt, :]

        hi, mid, low = _split3(c2)
        a = jnp.where(lane < 16, hi, jnp.where(lane < 32, pltpu.roll(mid, 16, 1),
                      jnp.where(lane < 48, pltpu.roll(low, 32, 1), jnp.where(lane == 48, 1.0, 0.0))))
        a = a.astype(BF16)
        q_spare = jnp.dot(a, eq_ref[...], preferred_element_type=F32)
        k_spare = jnp.dot(a, ek_ref[...], preferred_element_type=F32)

        for hp in range(npair):
            qp = _pair_norm(proj[:, hp * LANES:(hp + 1) * LANES], gains_ref[0:1, :])
            kp = _pair_norm(proj[:, (npair + hp) * LANES:(npair + hp + 1) * LANES], gains_ref[1:2, :])
            qs = (qp, pltpu.roll(qp, HEAD_DIM, 1))
            ks = (kp, pltpu.roll(kp, HEAD_DIM, 1))
            for half in range(2):
                hd = 2 * hp + half
                cols = slice(hd * LANES, (hd + 1) * LANES)
                ka_ref[hd, rows, :] = jnp.where(lo, ks[half], k_spare[:, cols]).astype(ka_ref.dtype)
                qa_ref[hd, rows, :] = jnp.where(lo, qs[half], q_spare[:, cols]).astype(qa_ref.dtype)

        for hd in range(2 * npair):
            vt_ref[hd, 0:HEAD_DIM, rows] = vt_all[hd * HEAD_DIM:(hd + 1) * HEAD_DIM, :].astype(vt_ref.dtype)
            vt_ref[hd, HEAD_DIM:VT_ROWS, rows] = ones_rows


def _spare_lane_placement(n_heads, shift):
    eq = np.zeros((LANES, n_heads * LANES), np.float32)
    ek = np.zeros((LANES, n_heads * LANES), np.float32)
    sh = np.zeros((LANES, n_heads * LANES), np.float32)
    for h in range(n_heads):
        base = h * LANES + HEAD_DIM
        for part in range(3):
            ek[16 * part + h, base + part] = 1.0
            eq[16 * part + h, base + 3 + part] = 1.0
        eq[48, base:base + 3] = -1.0
        ek[48, base + 3:base + 7] = 1.0
        sh[48, base + 6] = -1.0
    return (jnp.asarray(eq) + jnp.asarray(sh) * shift).astype(BF16), jnp.asarray(ek).astype(BF16)


def _in_odd(x2, g, w_qkf, w_vt, eq, ek, gains, bf, seq, tm, n_heads):
    n, d = x2.shape
    tiles_per_seq = seq // tm
    bsz = n // seq

    def bt(i):
        return i // tiles_per_seq, i % tiles_per_seq

    def whole(a):
        return pl.BlockSpec(a.shape, lambda i: (0, 0))

    return pl.pallas_call(
        functools.partial(_in_odd_kernel, tiles_per_seq=tiles_per_seq),
        grid=(n // tm,),
        in_specs=[pl.BlockSpec((tm, d), lambda i: (i, 0)), whole(g), whole(w_qkf), whole(w_vt),
                  whole(eq), whole(ek), whole(gains), whole(bf)],
        out_specs=[
            pl.BlockSpec((None, n_heads, tm, LANES), lambda i: (bt(i)[0], 0, bt(i)[1], 0)),
            pl.BlockSpec((None, n_heads, tm, LANES), lambda i: (bt(i)[0], 0, bt(i)[1], 0)),
            pl.BlockSpec((None, n_heads, None, VT_ROWS, tm), lambda i: (bt(i)[0], 0, bt(i)[1], 0, 0)),
            pl.BlockSpec((None, 1, LANES), lambda i: (i, 0, 0)),
            pl.BlockSpec((None, 1, LANES), lambda i: (i, 0, 0)),
        ],
        out_shape=[
            jax.ShapeDtypeStruct((bsz, n_heads, seq, LANES), BF16),
            jax.ShapeDtypeStruct((bsz, n_heads, seq, LANES), BF16),
            jax.ShapeDtypeStruct((bsz, n_heads, tiles_per_seq, VT_ROWS, tm), BF16),
            jax.ShapeDtypeStruct((n // tm, 1, LANES), F32),
            jax.ShapeDtypeStruct((n // tm, 1, LANES), F32),
        ],
        scratch_shapes=[pltpu.VMEM((8, LANES), F32)],
        compiler_params=_cparams(("arbitrary",)),
        name="in_odd",
    )(x2, g, w_qkf, w_vt, eq, ek, gains, bf)


def _fox_kernel(cfirst_ref, clast_ref, q_ref, k_ref, vt_ref, o_ref, m_ref, acc_ref,
                *, tq, kt, sub, depth, online_max):
    assert kt % sub == 0 and tq % kt == 0
    b = pl.program_id(0)
    pair = pl.program_id(1)
    i = pl.program_id(2)
    nsub = tq // sub
    per = tq // kt
    m_ref[...] = jnp.full(m_ref.shape, NEG, F32)
    acc_ref[...] = jnp.zeros(acc_ref.shape, F32)
    krow = lax.broadcasted_iota(jnp.int32, (kt, sub), 0)
    qcol = lax.broadcasted_iota(jnp.int32, (kt, sub), 1)
    krow_head = lax.broadcasted_iota(jnp.int32, (sub, sub), 0)
    qcol_head = lax.broadcasted_iota(jnp.int32, (sub, sub), 1)

    def scores(h, j, kb):
        qs = q_ref[h, j * sub:(j + 1) * sub, :]
        return lax.dot_general(kb, qs, (((1,), (1,)), ((), ())), preferred_element_type=F32)

    def update(h, j, s, vt, diag):
        idx = h * nsub + j
        if diag is not None:
            kr, qc = (krow, qcol) if s.shape[0] == kt else (krow_head, qcol_head)
            s = jnp.where(kr + diag <= qc, s, NEG)
        if online_max:
            m_old = m_ref[idx]
            m_new = jnp.maximum(m_old, jnp.max(s, axis=0, keepdims=True))
            alpha = jnp.exp2(m_old - m_new)
            p = jnp.exp2(s - m_new).astype(BF16)
            m_ref[idx] = m_new
            acc_ref[idx] = alpha * acc_ref[idx] + jnp.dot(vt, p, preferred_element_type=F32)
        else:
            acc_ref[idx] += jnp.dot(vt, jnp.exp2(s).astype(BF16), preferred_element_type=F32)

    def run(work):
        pending = [scores(*w[:3]) for w in work[:depth]]
        for t, (h, j, kb, vt, diag) in enumerate(work):
            if t + depth < len(work):
                pending.append(scores(*work[t + depth][:3]))
            update(h, j, pending.pop(0), vt, diag)

    def key_block(h, blk):
        return k_ref[h, pl.ds(pl.multiple_of(blk * kt, kt), kt), :]

    def body(kv2, carry):
        run([(h, j, key_block(h, kv), vt_ref[h, kv], None)
             for kv in (2 * kv2, 2 * kv2 + 1) for h in range(2) for j in range(nsub)])
        return carry

    first = 0
    if not online_max:
        def negligible(kv):
            gaps = [cfirst_ref[b, 2 * pair + h, i] - clast_ref[b, 2 * pair + h, kv] for h in range(2)]
            return jnp.maximum(gaps[0], gaps[1]) < -SKIP_LOG2

        first = lax.while_loop(lambda kv: jnp.logical_and(kv < i * per, negligible(kv)),
                               lambda kv: kv + 1, 0)
    assert per % 2 == 0
    lax.fori_loop(first // 2, (i * per) // 2, body, 0)

    work = []
    for h in range(2):
        for jk in range(per):
            kb = key_block(h, i * per + jk)
            vt = vt_ref[h, i * per + jk]
            kb_head = k_ref[h, pl.ds(pl.multiple_of((i * per + jk) * kt, kt), sub), :]
            vt_head = vt_ref[h, i * per + jk, :, 0:sub]
            for j in range(nsub):
                diag = jk * kt - j * sub
                if diag >= sub:
                    continue
                if diag >= 0:
                    work.append((h, j, kb_head, vt_head, diag))
                else:
                    work.append((h, j, kb, vt, diag if diag + kt - 1 > 0 else None))
    run(work)

    for j in range(nsub):
        outs = []
        for h in range(2):
            acc = acc_ref[h * nsub + j]
            outs.append(acc[0:HEAD_DIM, :] * (1.0 / acc[HEAD_DIM:HEAD_DIM + 1, :]))
        pair_t = jnp.concatenate(outs, axis=0)
        o_ref[j * sub:(j + 1) * sub, :] = pair_t.T.astype(o_ref.dtype)


def _fox(cfirst, clast, q_aug, k_aug, vt, tq, sub, online_max):
    bsz, n_heads, seq, _ = q_aug.shape
    nkt, kt = vt.shape[2], vt.shape[4]
    npair = n_heads // 2
    nsub = tq // sub
    return pl.pallas_call(
        functools.partial(_fox_kernel, tq=tq, kt=kt, sub=sub, depth=4, online_max=online_max),
        grid=(bsz, npair, seq // tq),
        in_specs=[
            pl.BlockSpec(memory_space=pltpu.SMEM),
            pl.BlockSpec(memory_space=pltpu.SMEM),
            pl.BlockSpec((None, 2, tq, LANES), lambda b, p, i: (b, p, i, 0)),
            pl.BlockSpec((None, 2, seq, LANES), lambda b, p, i: (b, p, 0, 0)),
            pl.BlockSpec((None, 2, nkt, VT_ROWS, kt), lambda b, p, i: (b, p, 0, 0, 0)),
        ],
        out_specs=pl.BlockSpec((None, tq, LANES), lambda b, p, i: (b, i, p)),
        out_shape=jax.ShapeDtypeStruct((bsz, seq, n_heads * HEAD_DIM), BF16),
        scratch_shapes=[pltpu.VMEM((2 * nsub, 1, sub), F32), pltpu.VMEM((2 * nsub, VT_ROWS, sub), F32)],
        compiler_params=_cparams(("parallel", "parallel", "arbitrary")),
        name="fox_online_max" if online_max else "fox",
    )(cfirst, clast, q_aug, k_aug, vt).reshape(bsz * seq, n_heads * HEAD_DIM)


def _pair_gain(gain, scale=1.0):
    return jnp.tile(gain.astype(F32) * scale, 2)


def _softmax_shift(q_gain, k_gain):
    bound = (HEAD_DIM ** 0.5 * LOG2E) * jnp.max(jnp.abs(q_gain)) * jnp.max(jnp.abs(k_gain))
    return (bound.astype(F32) * 1.02).astype(BF16).astype(F32)


def _forward(x, g_mix, g_mlp, w_in_even, a_q_gain, a_k_gain, b_q_gain, b_k_gain, b_sinks,
             w_out_even, w_in_odd, b_forget, c_q_gain, c_k_gain, w_out_odd, w_up, w_down):
    bsz, seq, d = x.shape
    n = bsz * seq
    depth = g_mix.shape[0]
    scale = HEAD_DIM ** -0.5
    n_heads_a = d // (2 * HEAD_DIM)
    n_heads_b = d // (2 * HEAD_DIM)
    n_heads_c = d // HEAD_DIM
    slopes = _alibi_slopes(n_heads_a + n_heads_b)
    x2 = x.reshape(n, d)
    tm = 512

    for layer in range(depth):
        i = layer // 2
        g1 = g_mix[layer].reshape(1, d)
        g2 = g_mlp[layer].reshape(1, d)
        wu = w_up[layer].astype(BF16)
        wd = w_down[layer].astype(BF16)
        if layer % 2 == 0:
            gains = jnp.stack([_pair_gain(a_q_gain[i], scale * LOG2E), _pair_gain(a_k_gain[i]),
                               _pair_gain(b_q_gain[i], scale * LOG2E), _pair_gain(b_k_gain[i])])
            nat, by4, by16 = _in_even(x2, g1, w_in_even[i].astype(BF16), gains, seq, tm)
            nat = nat.reshape(bsz, 1, seq, nat.shape[1])
            shift_a = _softmax_shift(a_q_gain[i], a_k_gain[i])
            shift_b = _softmax_shift(b_q_gain[i], b_k_gain[i])

            def attention(nat, by4, by16, sinks, fixed):
                dilated = dict(groups=(0, 1, 2), kcols=(0, 128, 256, 384), vcols=(0, 128, 256, 384),
                               max_dist=BLOCK, slopes=slopes[n_heads_b:], shift=shift_a, fixed=fixed)
                part = _banded(by16, dil=16, emit_lse=True, **dilated)
                part = _banded(by4, dil=4, prev=part, emit_lse=True, **dilated)
                a_out = _banded(nat, dil=1, prev=part, **dilated)
                b_out = _banded(nat, dil=1, groups=(3, 4, 4), kcols=(0, 0, 128, 128),
                                vcols=(256, 256, 384, 384), max_dist=WINDOW_B - 1, slopes=slopes[:n_heads_b],
                                shift=shift_b, fixed=fixed, sinks=sinks)
                return a_out.reshape(n, 512), b_out.reshape(n, 512)

            ys = lax.cond(jnp.maximum(shift_a, shift_b) <= MAX_FIXED_SHIFT,
                          lambda *a: attention(*a, fixed=True),
                          lambda *a: attention(*a, fixed=False),
                          nat, by4, by16, b_sinks[i].astype(F32))
            x2 = _post(list(ys), x2, w_out_even[i].astype(BF16), g2, wu, wd, tm)
        else:
            w = w_in_odd[i]
            w_qkf = jnp.pad(jnp.concatenate([w[:, :2 * d], w[:, 3 * d:]], axis=1),
                            ((0, 0), (0, LANES - n_heads_c))).astype(BF16)
            w_vt = w[:, 2 * d:3 * d].T.astype(BF16)
            shift = _softmax_shift(c_q_gain[i], c_k_gain[i])
            gains = jnp.stack([_pair_gain(c_q_gain[i], scale * LOG2E), _pair_gain(c_k_gain[i])])
            bf = jnp.pad(b_forget[i].astype(F32), (0, LANES - n_heads_c)).reshape(1, LANES)
            eq, ek = _spare_lane_placement(n_heads_c, shift)
            q_aug, k_aug, vt, cfirst, clast = _in_odd(x2, g1, w_qkf, w_vt, eq, ek, gains, bf, seq, tm, n_heads_c)
            tq = min(FOX_TQ, seq)
            cfirst = cfirst.reshape(bsz, seq // tm, LANES)[:, ::tq // tm, :n_heads_c].transpose(0, 2, 1)
            clast = clast.reshape(bsz, seq // tm, LANES)[:, :, :n_heads_c].transpose(0, 2, 1)
            fox = functools.partial(_fox, tq=tq, sub=256)
            y = lax.cond(shift <= MAX_FIXED_SHIFT,
                         lambda *a: fox(*a, online_max=False),
                         lambda *a: fox(*a, online_max=True),
                         cfirst, clast, q_aug, k_aug, vt)
            x2 = _post([y], x2, w_out_odd[i].astype(BF16), g2, wu, wd, tm)
    return x2.reshape(bsz, seq, d)


def kernel(x, g_mix, g_mlp, w_in_even, a_q_gain, a_k_gain, b_q_gain, b_k_gain, b_sinks, w_out_even,
           w_in_odd, b_forget, c_q_gain, c_k_gain, w_out_odd, w_up, w_down):
    return _forward(x, g_mix, g_mlp, w_in_even, a_q_gain, a_k_gain, b_q_gain, b_k_gain, b_sinks,
                    w_out_even, w_in_odd, b_forget, c_q_gain, c_k_gain, w_out_odd, w_up, w_down)
```

```python
import functools

import numpy as np
import jax
import jax.numpy as jnp
from jax import lax
from jax.experimental import pallas as pl
from jax.experimental.pallas import tpu as pltpu

F32 = jnp.float32
BF16 = jnp.bfloat16

HEAD_DIM = 64
LANES = 128
EPS = 1e-6
NEG = -1e30
BLOCK = 128
DILATIONS = (1, 4, 16)
WINDOW_B = 128
LOG2E = float(np.log2(np.e))
VT_ROWS = 80
ROW_GROUPS = 2
FOX_TQ = 2048
SKIP_LOG2 = 160.0
MAX_FIXED_SHIFT = 40.0
VMEM_LIMIT = 56 * 1024 * 1024


def _alibi_slopes(n):
    return np.asarray(2.0 ** (-8.0 * np.arange(1, n + 1) / n), dtype=np.float32)


def _cparams(sem):
    return pltpu.CompilerParams(dimension_semantics=sem, vmem_limit_bytes=VMEM_LIMIT)


def _rms(x, gain):
    return x * lax.rsqrt(jnp.mean(x * x, axis=-1, keepdims=True) + EPS) * gain


def _lo_mask(shape):
    return lax.broadcasted_iota(jnp.int32, shape, len(shape) - 1) < HEAD_DIM


def _pair_norm(xc, gain2):
    lo = _lo_mask(xc.shape)
    sq = xc * xc
    s_lo = jnp.sum(jnp.where(lo, sq, 0.0), axis=-1, keepdims=True)
    s_hi = jnp.sum(jnp.where(lo, 0.0, sq), axis=-1, keepdims=True)
    r = jnp.where(lo, lax.rsqrt(s_lo / HEAD_DIM + EPS), lax.rsqrt(s_hi / HEAD_DIM + EPS))
    return xc * r * gain2


def _in_even_kernel(x_ref, g_ref, w_ref, gains_ref, o_ref, o4_ref, o16_ref, s1_ref, s2_ref):
    tm = x_ref.shape[0]
    rt = tm // ROW_GROUPS
    for grp in range(ROW_GROUPS):
        _in_even_rows(grp, rt, x_ref, g_ref, w_ref, gains_ref, o_ref, o4_ref, o16_ref, s1_ref, s2_ref)


def _in_even_rows(grp, rt, x_ref, g_ref, w_ref, gains_ref, o_ref, o4_ref, o16_ref, s1_ref, s2_ref):
    rows = slice(grp * rt, (grp + 1) * rt)
    n4, n16 = rt // 4, rt // 16
    h = _rms(x_ref[rows, :], g_ref[...]).astype(BF16)
    proj = jnp.dot(h, w_ref[...], preferred_element_type=F32)
    lo = _lo_mask((rt, LANES))

    def col(c):
        return proj[:, c * LANES:(c + 1) * LANES]

    def put(c, val):
        o_ref[rows, c * LANES:(c + 1) * LANES] = val.astype(o_ref.dtype)
        if c >= 12:
            return
        cols = slice(c * LANES, (c + 1) * LANES)
        s1_ref[grp, c] = val
        for r in range(4):
            part = s1_ref[grp, c, pl.ds(r, n4, stride=4), :]
            o4_ref[r, grp * n4:(grp + 1) * n4, cols] = part.astype(o4_ref.dtype)
            s2_ref[grp, c, r * n4:(r + 1) * n4, :] = part
        for r in range(4):
            for r2 in range(4):
                part = s2_ref[grp, c, pl.ds(r * n4 + r2, n16, stride=4), :]
                o16_ref[4 * r + r2, grp * n16:(grp + 1) * n16, cols] = part.astype(o16_ref.dtype)

    for c in range(4):
        put(c, _pair_norm(col(c), gains_ref[0:1, :]))
    for c in range(4, 8):
        put(c, _pair_norm(col(c), gains_ref[1:2, :]))
    for c in range(8, 12):
        put(c, col(c))
    for c in range(12, 16):
        put(c, _pair_norm(col(c), gains_ref[2:3, :]))
    bk = _pair_norm(col(16), gains_ref[3:4, :])
    bkr = pltpu.roll(bk, HEAD_DIM, 1)
    put(16, jnp.where(lo, bk, bkr))
    put(17, jnp.where(lo, bkr, bk))
    bv = col(17)
    bvr = pltpu.roll(bv, HEAD_DIM, 1)
    put(18, jnp.where(lo, bv, bvr))
    put(19, jnp.where(lo, bvr, bv))


def _in_even(x2, g, w, gains, seq, tm):
    n, d = x2.shape
    e = w.shape[1]
    bsz = n // seq
    tiles_per_seq = seq // tm
    a_width = 12 * LANES

    def regrouped(dil):
        return pl.BlockSpec((None, dil, tm // dil, a_width),
                            lambda i: (i // tiles_per_seq, 0, i % tiles_per_seq, 0))

    return pl.pallas_call(
        _in_even_kernel,
        grid=(n // tm,),
        in_specs=[
            pl.BlockSpec((tm, d), lambda i: (i, 0)),
            pl.BlockSpec((1, d), lambda i: (0, 0)),
            pl.BlockSpec((d, e), lambda i: (0, 0)),
            pl.BlockSpec((4, LANES), lambda i: (0, 0)),
        ],
        out_specs=[pl.BlockSpec((tm, 20 * LANES), lambda i: (i, 0)), regrouped(4), regrouped(16)],
        out_shape=[jax.ShapeDtypeStruct((n, 20 * LANES), BF16),
                   jax.ShapeDtypeStruct((bsz, 4, seq // 4, a_width), BF16),
                   jax.ShapeDtypeStruct((bsz, 16, seq // 16, a_width), BF16)],
        scratch_shapes=[pltpu.VMEM((ROW_GROUPS, 12, tm // ROW_GROUPS, LANES), F32)] * 2,
        compiler_params=_cparams(("parallel",)),
        name="in_even",
    )(x2, g, w, gains)


def _banded_kernel(*refs, tu, dil, max_dist, slopes, kcols, vcols, use_sink, merge_in, emit_lse,
                   lookahead=4):
    refs = list(refs)
    scal_ref, q_ref, kc_ref, kp_ref, vc_ref, vp_ref = refs[:6]
    refs = refs[6:]
    if merge_in:
        po_ref, pl_ref = refs[:2]
        refs = refs[2:]
    o_ref = refs.pop(0)
    lse_ref = refs.pop(0) if emit_lse else None
    if merge_in:
        mo_ref, ml_ref = refs
        n4 = tu // 4
        for slab in range(4):
            cs = slice(slab * LANES, (slab + 1) * LANES)
            for r2 in range(4):
                mo_ref[slab, pl.ds(r2, n4, stride=4), :] = po_ref[r2, :, cs]
                ml_ref[slab, pl.ds(r2, n4, stride=4), :] = pl_ref[r2, :, cs]
    first = pl.program_id(2) == 0
    nj = tu // BLOCK
    lo = _lo_mask((BLOCK, LANES))
    qi = lax.broadcasted_iota(jnp.int32, (BLOCK, 2 * BLOCK), 0)
    kj = lax.broadcasted_iota(jnp.int32, (BLOCK, 2 * BLOCK), 1)
    dist = qi + BLOCK - kj
    valid = jnp.logical_and(dist >= 0, dist <= max_dist)
    valid_first = jnp.logical_and(valid, jnp.logical_or(kj >= BLOCK, jnp.logical_not(first)))
    distf = (dist * dil).astype(F32)

    n_heads = 2 * len(kcols)
    biases = []
    for head in range(n_heads):
        bias = -(slopes[head] * LOG2E) * distf
        biases.append((jnp.where(valid_first, bias, NEG), jnp.where(valid, bias, NEG)))

    def window(cur_ref, halo_ref, j, c0):
        if j == 0:
            return jnp.concatenate([halo_ref[:, c0:c0 + LANES], cur_ref[0:BLOCK, c0:c0 + LANES]], axis=0)
        return cur_ref[(j - 1) * BLOCK:(j + 1) * BLOCK, c0:c0 + LANES]

    def scores(hp, j, half):
        q2 = q_ref[j * BLOCK:(j + 1) * BLOCK, hp * LANES:(hp + 1) * LANES]
        qm = jnp.where(lo, q2, 0) if half == 0 else jnp.where(lo, 0, q2)
        kw = window(kc_ref, kp_ref, j, kcols[hp])
        return lax.dot_general(qm, kw, (((1,), (1,)), ((), ())), preferred_element_type=F32)

    halves = {}

    def finish(hp, j, half, s):
        s = s + biases[2 * hp + half][0 if j == 0 else 1]
        m = jnp.max(s, axis=-1, keepdims=True)
        if use_sink:
            sink = scal_ref[1 + 2 * hp + half]
            m = jnp.maximum(m, sink)
        p = jnp.exp2(s - m)
        l = jnp.sum(p, axis=-1, keepdims=True)
        if use_sink:
            l = l + jnp.exp2(sink - m)
        vw = window(vc_ref, vp_ref, j, vcols[hp])
        pv = jnp.dot(p.astype(BF16), vw, preferred_element_type=F32)
        halves[half] = (pv * (1.0 / l), (m + jnp.log2(l)) * (1.0 / LOG2E))
        if half == 0:
            return
        rows = slice(j * BLOCK, (j + 1) * BLOCK)
        cols = slice(hp * LANES, (hp + 1) * LANES)
        o_blk = jnp.where(lo, halves[0][0], halves[1][0])
        l_blk = jnp.where(lo, halves[0][1], halves[1][1])
        if merge_in:
            o_prev, l_prev = mo_ref[hp, rows, :], ml_ref[hp, rows, :]
            mx = jnp.maximum(l_blk, l_prev)
            w_own, w_prev = jnp.exp(l_blk - mx), jnp.exp(l_prev - mx)
            den = w_own + w_prev
            o_blk = (w_own * o_blk + w_prev * o_prev) * (1.0 / den)
            l_blk = mx + jnp.log(den)
        o_ref[rows, cols] = o_blk.astype(o_ref.dtype)
        if emit_lse:
            lse_ref[rows, cols] = l_blk

    items = [(hp, j, half) for hp in range(len(kcols)) for j in range(nj) for half in range(2)]
    pending = [scores(*it) for it in items[:lookahead]]
    for t, it in enumerate(items):
        if t + lookahead < len(items):
            pending.append(scores(*items[t + lookahead]))
        finish(*it, pending.pop(0))


def _banded_tables(slopes, dil, shift, qgroup):
    def split3(x):
        x = np.asarray(x, np.float32)
        hi = x.astype(jnp.bfloat16).astype(np.float32)
        mid = (x - hi).astype(jnp.bfloat16).astype(np.float32)
        return hi, mid, x - hi - mid

    wk = qgroup + BLOCK
    krow = np.arange(wk)
    k1, k2 = (krow & ~1).astype(np.float32), (krow & 1).astype(np.float32)
    ktab = np.zeros((2, wk, LANES), np.float32)
    qconst = np.zeros((len(slopes), qgroup, LANES), np.float32)
    qshift = np.zeros((len(slopes), qgroup, LANES), np.float32)
    for half in range(2):
        base = HEAD_DIM * (1 - half)
        ktab[half, :, base:base + 3] = k1[:, None]
        ktab[half, :, base + 3:base + 6] = k2[:, None]
        ktab[half, :, base + 6:base + 10] = 1.0
    for head, slope in enumerate(slopes):
        base = HEAD_DIM * (1 - head % 2)
        sigma = np.float32(slope) * np.float32(dil) * np.float32(LOG2E)
        s3 = split3(sigma)
        t3 = split3(-sigma * (np.arange(qgroup, dtype=np.float32) + BLOCK))
        for part in range(3):
            qconst[head, :, base + part] = s3[part]
            qconst[head, :, base + 3 + part] = s3[part]
            qconst[head, :, base + 6 + part] = t3[part]
        qshift[head, :, base + 9] = -1.0
    qtab = (jnp.asarray(qconst) + jnp.asarray(qshift) * shift).astype(BF16)
    return qtab, jnp.asarray(ktab).astype(BF16)


def _banded_fixed_kernel(*refs, tu, qgroup, max_dist, kcols, vcols, use_sink, merge_in, emit_lse,
                         lookahead=3):
    refs = list(refs)
    scal_ref, qtab_ref, ktab_ref, q_ref, kc_ref, kp_ref, vc_ref, vp_ref = refs[:8]
    refs = refs[8:]
    if merge_in:
        po_ref, pl_ref = refs[:2]
        refs = refs[2:]
    o_ref = refs.pop(0)
    lse_ref = refs.pop(0) if emit_lse else None
    if merge_in:
        mo_ref, ml_ref = refs
        n4 = tu // 4
        for slab in range(4):
            cs = slice(slab * LANES, (slab + 1) * LANES)
            for r2 in range(4):
                mo_ref[slab, pl.ds(r2, n4, stride=4), :] = po_ref[r2, :, cs]
                ml_ref[slab, pl.ds(r2, n4, stride=4), :] = pl_ref[r2, :, cs]
    first = pl.program_id(2) == 0
    shift = scal_ref[0]
    wk = qgroup + BLOCK
    ngroups = tu // qgroup
    krow = lax.broadcasted_iota(jnp.int32, (wk, qgroup), 0)
    qcol = lax.broadcasted_iota(jnp.int32, (wk, qgroup), 1)
    dist = qcol + BLOCK - krow
    valid = jnp.logical_and(dist >= 0, dist <= max_dist)
    valid_first = jnp.logical_and(valid, jnp.logical_or(krow >= BLOCK, jnp.logical_not(first)))
    lo_q = _lo_mask((qgroup, LANES))
    lo_k = _lo_mask((wk, LANES))
    ones_rows = (lax.broadcasted_iota(jnp.int32, (VT_ROWS - HEAD_DIM, wk), 0) == 0).astype(BF16)

    def window(cur_ref, halo_ref, g, c0):
        if g == 0:
            return jnp.concatenate([halo_ref[:, c0:c0 + LANES], cur_ref[0:qgroup, c0:c0 + LANES]], axis=0)
        return cur_ref[g * qgroup - BLOCK:(g + 1) * qgroup, c0:c0 + LANES]

    def scores(hp, g, half):
        q2 = q_ref[g * qgroup:(g + 1) * qgroup, hp * LANES:(hp + 1) * LANES]
        kw = window(kc_ref, kp_ref, g, kcols[hp])
        own_q = lo_q if half == 0 else jnp.logical_not(lo_q)
        own_k = lo_k if half == 0 else jnp.logical_not(lo_k)
        q_aug = jnp.where(own_q, q2, qtab_ref[2 * hp + half])
        k_aug = jnp.where(own_k, kw, ktab_ref[half])
        return lax.dot_general(k_aug, q_aug, (((1,), (1,)), ((), ())), preferred_element_type=F32)

    state = {}

    def finish(hp, g, half, s):
        s = jnp.where(valid_first if g == 0 else valid, s, NEG)
        p = jnp.exp2(s).astype(BF16)
        if half == 0:
            vw = window(vc_ref, vp_ref, g, vcols[hp])
            state["vt"] = vw.astype(F32).T
        vt = state["vt"][half * HEAD_DIM:(half + 1) * HEAD_DIM, :].astype(BF16)
        acc = jnp.dot(jnp.concatenate([vt, ones_rows], axis=0), p, preferred_element_type=F32)
        l = acc[HEAD_DIM:HEAD_DIM + 1, :]
        if use_sink:
            l = l + jnp.exp2(scal_ref[1 + 2 * hp + half] - shift)
        state[half] = (acc[0:HEAD_DIM, :] * (1.0 / l), l)
        if half == 0:
            return
        rows = slice(g * qgroup, (g + 1) * qgroup)
        cols = slice(hp * LANES, (hp + 1) * LANES)
        o_blk = jnp.concatenate([state[0][0], state[1][0]], axis=0).T
        if merge_in or emit_lse:
            lse_t = jnp.concatenate(
                [jnp.broadcast_to((jnp.log2(state[hf][1]) + shift) * (1.0 / LOG2E), (HEAD_DIM, qgroup))
                 for hf in range(2)], axis=0)
            l_blk = lse_t.T
        if merge_in:
            o_prev, l_prev = mo_ref[hp, rows, :], ml_ref[hp, rows, :]
            mx = jnp.maximum(l_blk, l_prev)
            w_own, w_prev = jnp.exp(l_blk - mx), jnp.exp(l_prev - mx)
            den = w_own + w_prev
            o_blk = (w_own * o_blk + w_prev * o_prev) * (1.0 / den)
            l_blk = mx + jnp.log(den)
        o_ref[rows, cols] = o_blk.astype(o_ref.dtype)
        if emit_lse:
            lse_ref[rows, cols] = l_blk

    items = [(hp, g, half) for hp in range(len(kcols)) for g in range(ngroups) for half in range(2)]
    pending = [scores(*it) for it in items[:lookahead]]
    for t, it in enumerate(items):
        if t + lookahead < len(items):
            pending.append(scores(*items[t + lookahead]))
        finish(*it, pending.pop(0))


def _banded(src, *, dil, groups, kcols, vcols, max_dist, slopes, shift, fixed, sinks=None, prev=None,
            emit_lse=False):
    bsz, _, u, width = src.shape
    tu = min(512, u)
    per = tu // BLOCK

    def spec(grp, halo):
        if halo:
            return pl.BlockSpec((None, None, BLOCK, 512),
                                lambda b, r, i: (b, r, jnp.maximum(i * per - 1, 0), grp))
        return pl.BlockSpec((None, None, tu, 512), lambda b, r, i: (b, r, i, grp))

    qg, kg, vg = groups
    in_specs = [spec(qg, False), spec(kg, False), spec(kg, True), spec(vg, False), spec(vg, True)]
    args = [src] * 5
    use_sink = sinks is not None
    scal = jnp.reshape(shift, (1,)).astype(F32)
    if use_sink:
        scal = jnp.concatenate([scal, sinks.astype(F32) * LOG2E])
    qgroup = min(2 * BLOCK, tu)
    if fixed:
        qtab, ktab = _banded_tables(slopes, dil, shift, qgroup)
        in_specs = [pl.BlockSpec(qtab.shape, lambda b, r, i: (0, 0, 0)),
                    pl.BlockSpec(ktab.shape, lambda b, r, i: (0, 0, 0))] + in_specs
        args = [qtab, ktab] + args
    in_specs = [pl.BlockSpec(memory_space=pltpu.SMEM)] + in_specs
    args = [scal] + args
    scratch = []
    if prev is not None:
        in_specs += [pl.BlockSpec((None, 4, tu // 4, 512), lambda b, r, i: (b, r, i, 0))] * 2
        args += list(prev)
        scratch = [pltpu.VMEM((4, tu, LANES), F32)] * 2
    out_spec = pl.BlockSpec((None, None, tu, 512), lambda b, r, i: (b, r, i, 0))
    if emit_lse:
        out_specs = [out_spec, out_spec]
        out_shape = [jax.ShapeDtypeStruct((bsz, dil, u, 512), F32)] * 2
    else:
        out_specs = out_spec
        out_shape = jax.ShapeDtypeStruct((bsz, dil, u, 512), BF16)
    common = dict(tu=tu, max_dist=max_dist, kcols=kcols, vcols=vcols, use_sink=use_sink,
                  merge_in=prev is not None, emit_lse=emit_lse)
    if fixed:
        kern = functools.partial(_banded_fixed_kernel, qgroup=qgroup, **common)
    else:
        kern = functools.partial(_banded_kernel, dil=dil, slopes=tuple(float(s) for s in slopes), **common)
    return pl.pallas_call(
        kern,
        grid=(bsz, dil, u // tu),
        in_specs=in_specs,
        out_specs=out_specs,
        out_shape=out_shape,
        scratch_shapes=scratch,
        compiler_params=_cparams(("parallel", "parallel", "arbitrary")),
        name=f"banded_d{dil}" + ("_sink" if use_sink else "") + ("" if fixed else "_online_max"),
    )(*args)


def _post_kernel(*refs, n_y, ff_chunk):
    y_refs = refs[:n_y]
    x_ref, wo_ref, g_ref, wu_ref, wd_ref, out_ref = refs[n_y:]
    rt = x_ref.shape[0] // ROW_GROUPS
    groups = [slice(grp * rt, (grp + 1) * rt) for grp in range(ROW_GROUPS)]
    x1s = []
    for rows in groups:
        y = y_refs[0][rows, :] if n_y == 1 else jnp.concatenate([r[rows, :] for r in y_refs], axis=1)
        x1s.append(x_ref[rows, :] + jnp.dot(y, wo_ref[...], preferred_element_type=F32))
    d_ff = wu_ref.shape[1]
    for rows, x1 in zip(groups, x1s):
        h = _rms(x1, g_ref[...]).astype(BF16)
        acc = x1
        for f in range(d_ff // ff_chunk):
            cs = slice(f * ff_chunk, (f + 1) * ff_chunk)
            u = jnp.maximum(jnp.dot(h, wu_ref[:, cs], preferred_element_type=F32), 0.0)
            acc = acc + jnp.dot((u * u).astype(BF16), wd_ref[cs, :], preferred_element_type=F32)
        out_ref[rows, :] = acc


def _post(ys, x2, w_out, g, w_up, w_down, tm):
    n, d = x2.shape
    d_ff = w_up.shape[1]

    def rows(width):
        return pl.BlockSpec((tm, width), lambda i: (i, 0))

    def whole(shape):
        return pl.BlockSpec(shape, lambda i: (0, 0), pipeline_mode=pl.Buffered(1))

    in_specs = [rows(y.shape[1]) for y in ys] + [
        rows(d), whole((d, d)), whole((1, d)), whole((d, d_ff)), whole((d_ff, d))]
    return pl.pallas_call(
        functools.partial(_post_kernel, n_y=len(ys), ff_chunk=1024),
        grid=(n // tm,),
        in_specs=in_specs,
        out_specs=rows(d),
        out_shape=jax.ShapeDtypeStruct((n, d), F32),
        compiler_params=_cparams(("parallel",)),
        name=f"post_{len(ys)}",
    )(*ys, x2, w_out, g, w_up, w_down)


def _split3(x):
    hi = x.astype(BF16).astype(F32)
    r = x - hi
    mid = r.astype(BF16).astype(F32)
    return hi, mid, r - mid


def _in_odd_kernel(x_ref, g_ref, w_ref, wvt_ref, eq_ref, ek_ref, gains_ref, bf_ref,
                   qa_ref, ka_ref, vt_ref, cfirst_ref, clast_ref, carry_ref, *, tiles_per_seq):
    i = pl.program_id(0)
    tm = x_ref.shape[0]
    rt = tm // ROW_GROUPS
    d2 = w_ref.shape[1] - LANES
    npair = d2 // (2 * LANES)

    @pl.when(i % tiles_per_seq == 0)
    def _():
        carry_ref[...] = jnp.zeros_like(carry_ref)

    r = lax.broadcasted_iota(jnp.int32, (rt, rt), 0)
    c = lax.broadcasted_iota(jnp.int32, (rt, rt), 1)
    tri = (c <= r).astype(F32)
    lane = lax.broadcasted_iota(jnp.int32, (rt, LANES), 1)
    lo = lane < HEAD_DIM
    ones_rows = (lax.broadcasted_iota(jnp.int32, (VT_ROWS - HEAD_DIM, rt), 0) == 0).astype(vt_ref.dtype)

    for grp in range(ROW_GROUPS):
        rows = slice(grp * rt, (grp + 1) * rt)
        h = _rms(x_ref[rows, :], g_ref[...]).astype(BF16)
        z = jnp.dot(h, w_ref[:, d2:], preferred_element_type=F32) + bf_ref[...]
        proj = jnp.dot(h, w_ref[:, :d2], preferred_element_type=F32)
        vt_all = lax.dot_general(wvt_ref[...], h, (((1,), (1,)), ((), ())),
                                 preferred_element_type=F32)
        log_f = jnp.minimum(z, 0.0) - jnp.log1p(jnp.exp(-jnp.abs(z)))
        local = jnp.dot(tri, log_f, preferred_element_type=F32, precision=lax.Precision.HIGHEST)
        csum = local + carry_ref[0:1, :]
        carry_ref[0:1, :] = csum[rt - 1:rt, :]
        c2 = csum * LOG2E
        if grp == 0:
            cfirst_ref[...] = c2[0:1, :]
        if grp == ROW_GROUPS - 1:
            clast_ref[...] = c2[rt - 1:rt, :]

        hi, mid, low = _split3(c2)
        a = jnp.where(lane < 16, hi, jnp.where(lane < 32, pltpu.roll(mid, 16, 1),
                      jnp.where(lane < 48, pltpu.roll(low, 32, 1), jnp.where(lane == 48, 1.0, 0.0))))
        a = a.astype(BF16)
        q_spare = jnp.dot(a, eq_ref[...], preferred_element_type=F32)
        k_spare = jnp.dot(a, ek_ref[...], preferred_element_type=F32)

        for hp in range(npair):
            qp = _pair_norm(proj[:, hp * LANES:(hp + 1) * LANES], gains_ref[0:1, :])
            kp = _pair_norm(proj[:, (npair + hp) * LANES:(npair + hp + 1) * LANES], gains_ref[1:2, :])
            qs = (qp, pltpu.roll(qp, HEAD_DIM, 1))
            ks = (kp, pltpu.roll(kp, HEAD_DIM, 1))
            for half in range(2):
                hd = 2 * hp + half
                cols = slice(hd * LANES, (hd + 1) * LANES)
                ka_ref[hd, rows, :] = jnp.where(lo, ks[half], k_spare[:, cols]).astype(ka_ref.dtype)
                qa_ref[hd, rows, :] = jnp.where(lo, qs[half], q_spare[:, cols]).astype(qa_ref.dtype)

        for hd in range(2 * npair):
            vt_ref[hd, 0:HEAD_DIM, rows] = vt_all[hd * HEAD_DIM:(hd + 1) * HEAD_DIM, :].astype(vt_ref.dtype)
            vt_ref[hd, HEAD_DIM:VT_ROWS, rows] = ones_rows


def _spare_lane_placement(n_heads, shift):
    eq = np.zeros((LANES, n_heads * LANES), np.float32)
    ek = np.zeros((LANES, n_heads * LANES), np.float32)
    sh = np.zeros((LANES, n_heads * LANES), np.float32)
    for h in range(n_heads):
        base = h * LANES + HEAD_DIM
        for part in range(3):
            ek[16 * part + h, base + part] = 1.0
            eq[16 * part + h, base + 3 + part] = 1.0
        eq[48, base:base + 3] = -1.0
        ek[48, base + 3:base + 7] = 1.0
        sh[48, base + 6] = -1.0
    return (jnp.asarray(eq) + jnp.asarray(sh) * shift).astype(BF16), jnp.asarray(ek).astype(BF16)


def _in_odd(x2, g, w_qkf, w_vt, eq, ek, gains, bf, seq, tm, n_heads):
    n, d = x2.shape
    tiles_per_seq = seq // tm
    bsz = n // seq

    def bt(i):
        return i // tiles_per_seq, i % tiles_per_seq

    def whole(a):
        return pl.BlockSpec(a.shape, lambda i: (0, 0))

    return pl.pallas_call(
        functools.partial(_in_odd_kernel, tiles_per_seq=tiles_per_seq),
        grid=(n // tm,),
        in_specs=[pl.BlockSpec((tm, d), lambda i: (i, 0)), whole(g), whole(w_qkf), whole(w_vt),
                  whole(eq), whole(ek), whole(gains), whole(bf)],
        out_specs=[
            pl.BlockSpec((None, n_heads, tm, LANES), lambda i: (bt(i)[0], 0, bt(i)[1], 0)),
            pl.BlockSpec((None, n_heads, tm, LANES), lambda i: (bt(i)[0], 0, bt(i)[1], 0)),
            pl.BlockSpec((None, n_heads, None, VT_ROWS, tm), lambda i: (bt(i)[0], 0, bt(i)[1], 0, 0)),
            pl.BlockSpec((None, 1, LANES), lambda i: (i, 0, 0)),
            pl.BlockSpec((None, 1, LANES), lambda i: (i, 0, 0)),
        ],
        out_shape=[
            jax.ShapeDtypeStruct((bsz, n_heads, seq, LANES), BF16),
            jax.ShapeDtypeStruct((bsz, n_heads, seq, LANES), BF16),
            jax.ShapeDtypeStruct((bsz, n_heads, tiles_per_seq, VT_ROWS, tm), BF16),
            jax.ShapeDtypeStruct((n // tm, 1, LANES), F32),
            jax.ShapeDtypeStruct((n // tm, 1, LANES), F32),
        ],
        scratch_shapes=[pltpu.VMEM((8, LANES), F32)],
        compiler_params=_cparams(("arbitrary",)),
        name="in_odd",
    )(x2, g, w_qkf, w_vt, eq, ek, gains, bf)


def _fox_kernel(cfirst_ref, clast_ref, q_ref, k_ref, vt_ref, o_ref, m_ref, acc_ref, qt_ref,
                *, tq, kt, sub, depth, online_max):
    assert kt % sub == 0 and tq % kt == 0
    b = pl.program_id(0)
    pair = pl.program_id(1)
    i = pl.program_id(2)
    nsub = tq // sub
    per = tq // kt
    m_ref[...] = jnp.full(m_ref.shape, NEG, F32)
    acc_ref[...] = jnp.zeros(acc_ref.shape, F32)
    krow = lax.broadcasted_iota(jnp.int32, (kt, sub), 0)
    qcol = lax.broadcasted_iota(jnp.int32, (kt, sub), 1)
    krow_head = lax.broadcasted_iota(jnp.int32, (sub, sub), 0)
    qcol_head = lax.broadcasted_iota(jnp.int32, (sub, sub), 1)

    for h in range(2):
        for j in range(nsub):
            qt_ref[h * nsub + j] = q_ref[h, j * sub:(j + 1) * sub, :].astype(F32).T.astype(BF16)

    def scores(h, j, kb):
        return jnp.dot(kb, qt_ref[h * nsub + j], preferred_element_type=F32)

    def update(h, j, s, vt, diag):
        idx = h * nsub + j
        if diag is not None:
            kr, qc = (krow, qcol) if s.shape[0] == kt else (krow_head, qcol_head)
            s = jnp.where(kr + diag <= qc, s, NEG)
        if online_max:
            m_old = m_ref[idx]
            m_new = jnp.maximum(m_old, jnp.max(s, axis=0, keepdims=True))
            alpha = jnp.exp2(m_old - m_new)
            p = jnp.exp2(s - m_new).astype(BF16)
            m_ref[idx] = m_new
            acc_ref[idx] = alpha * acc_ref[idx] + jnp.dot(vt, p, preferred_element_type=F32)
        else:
            acc_ref[idx] += jnp.dot(vt, jnp.exp2(s).astype(BF16), preferred_element_type=F32)

    def run(work):
        pending = [scores(*w[:3]) for w in work[:depth]]
        for t, (h, j, kb, vt, diag) in enumerate(work):
            if t + depth < len(work):
                pending.append(scores(*work[t + depth][:3]))
            update(h, j, pending.pop(0), vt, diag)

    def key_block(h, blk):
        return k_ref[h, pl.ds(pl.multiple_of(blk * kt, kt), kt), :]

    def body(kv2, carry):
        run([(h, j, key_block(h, kv), vt_ref[h, kv], None)
             for kv in (2 * kv2, 2 * kv2 + 1) for h in range(2) for j in range(nsub)])
        return carry

    first = 0
    if not online_max:
        def negligible(kv):
            gaps = [cfirst_ref[b, 2 * pair + h, i] - clast_ref[b, 2 * pair + h, kv] for h in range(2)]
            return jnp.maximum(gaps[0], gaps[1]) < -SKIP_LOG2

        first = lax.while_loop(lambda kv: jnp.logical_and(kv < i * per, negligible(kv)),
                               lambda kv: kv + 1, 0)
    assert per % 2 == 0
    lax.fori_loop(first // 2, (i * per) // 2, body, 0)

    work = []
    for h in range(2):
        for jk in range(per):
            kb = key_block(h, i * per + jk)
            vt = vt_ref[h, i * per + jk]
            kb_head = k_ref[h, pl.ds(pl.multiple_of((i * per + jk) * kt, kt), sub), :]
            vt_head = vt_ref[h, i * per + jk, :, 0:sub]
            for j in range(nsub):
                diag = jk * kt - j * sub
                if diag >= sub:
                    continue
                if diag >= 0:
                    work.append((h, j, kb_head, vt_head, diag))
                else:
                    work.append((h, j, kb, vt, diag if diag + kt - 1 > 0 else None))
    run(work)

    for j in range(nsub):
        outs = []
        for h in range(2):
            acc = acc_ref[h * nsub + j]
            outs.append(acc[0:HEAD_DIM, :] * (1.0 / acc[HEAD_DIM:HEAD_DIM + 1, :]))
        pair_t = jnp.concatenate(outs, axis=0)
        o_ref[j * sub:(j + 1) * sub, :] = pair_t.T.astype(o_ref.dtype)


def _fox(cfirst, clast, q_aug, k_aug, vt, tq, sub, online_max):
    bsz, n_heads, seq, _ = q_aug.shape
    nkt, kt = vt.shape[2], vt.shape[4]
    npair = n_heads // 2
    nsub = tq // sub
    return pl.pallas_call(
        functools.partial(_fox_kernel, tq=tq, kt=kt, sub=sub, depth=4, online_max=online_max),
        grid=(bsz, npair, seq // tq),
        in_specs=[
            pl.BlockSpec(memory_space=pltpu.SMEM),
            pl.BlockSpec(memory_space=pltpu.SMEM),
            pl.BlockSpec((None, 2, tq, LANES), lambda b, p, i: (b, p, i, 0)),
            pl.BlockSpec((None, 2, seq, LANES), lambda b, p, i: (b, p, 0, 0)),
            pl.BlockSpec((None, 2, nkt, VT_ROWS, kt), lambda b, p, i: (b, p, 0, 0, 0)),
        ],
        out_specs=pl.BlockSpec((None, tq, LANES), lambda b, p, i: (b, i, p)),
        out_shape=jax.ShapeDtypeStruct((bsz, seq, n_heads * HEAD_DIM), BF16),
        scratch_shapes=[pltpu.VMEM((2 * nsub, 1, sub), F32), pltpu.VMEM((2 * nsub, VT_ROWS, sub), F32),
                        pltpu.VMEM((2 * nsub, LANES, sub), BF16)],
        compiler_params=_cparams(("parallel", "parallel", "arbitrary")),
        name="fox_online_max" if online_max else "fox",
    )(cfirst, clast, q_aug, k_aug, vt).reshape(bsz * seq, n_heads * HEAD_DIM)


def _pair_gain(gain, scale=1.0):
    return jnp.tile(gain.astype(F32) * scale, 2)


def _softmax_shift(q_gain, k_gain):
    bound = (HEAD_DIM ** 0.5 * LOG2E) * jnp.max(jnp.abs(q_gain)) * jnp.max(jnp.abs(k_gain))
    return (bound.astype(F32) * 1.02).astype(BF16).astype(F32)


def _forward(x, g_mix, g_mlp, w_in_even, a_q_gain, a_k_gain, b_q_gain, b_k_gain, b_sinks,
             w_out_even, w_in_odd, b_forget, c_q_gain, c_k_gain, w_out_odd, w_up, w_down):
    bsz, seq, d = x.shape
    n = bsz * seq
    depth = g_mix.shape[0]
    scale = HEAD_DIM ** -0.5
    n_heads_a = d // (2 * HEAD_DIM)
    n_heads_b = d // (2 * HEAD_DIM)
    n_heads_c = d // HEAD_DIM
    slopes = _alibi_slopes(n_heads_a + n_heads_b)
    x2 = x.reshape(n, d)
    tm = 512

    for layer in range(depth):
        i = layer // 2
        g1 = g_mix[layer].reshape(1, d)
        g2 = g_mlp[layer].reshape(1, d)
        wu = w_up[layer].astype(BF16)
        wd = w_down[layer].astype(BF16)
        if layer % 2 == 0:
            gains = jnp.stack([_pair_gain(a_q_gain[i], scale * LOG2E), _pair_gain(a_k_gain[i]),
                               _pair_gain(b_q_gain[i], scale * LOG2E), _pair_gain(b_k_gain[i])])
            nat, by4, by16 = _in_even(x2, g1, w_in_even[i].astype(BF16), gains, seq, tm)
            nat = nat.reshape(bsz, 1, seq, nat.shape[1])
            shift_a = _softmax_shift(a_q_gain[i], a_k_gain[i])
            shift_b = _softmax_shift(b_q_gain[i], b_k_gain[i])

            def attention(nat, by4, by16, sinks, fixed):
                dilated = dict(groups=(0, 1, 2), kcols=(0, 128, 256, 384), vcols=(0, 128, 256, 384),
                               max_dist=BLOCK, slopes=slopes[n_heads_b:], shift=shift_a, fixed=fixed)
                part = _banded(by16, dil=16, emit_lse=True, **dilated)
                part = _banded(by4, dil=4, prev=part, emit_lse=True, **dilated)
                a_out = _banded(nat, dil=1, prev=part, **dilated)
                b_out = _banded(nat, dil=1, groups=(3, 4, 4), kcols=(0, 0, 128, 128),
                                vcols=(256, 256, 384, 384), max_dist=WINDOW_B - 1, slopes=slopes[:n_heads_b],
                                shift=shift_b, fixed=fixed, sinks=sinks)
                return a_out.reshape(n, 512), b_out.reshape(n, 512)

            ys = lax.cond(jnp.maximum(shift_a, shift_b) <= MAX_FIXED_SHIFT,
                          lambda *a: attention(*a, fixed=True),
                          lambda *a: attention(*a, fixed=False),
                          nat, by4, by16, b_sinks[i].astype(F32))
            x2 = _post(list(ys), x2, w_out_even[i].astype(BF16), g2, wu, wd, tm)
        else:
            w = w_in_odd[i]
            w_qkf = jnp.pad(jnp.concatenate([w[:, :2 * d], w[:, 3 * d:]], axis=1),
                            ((0, 0), (0, LANES - n_heads_c))).astype(BF16)
            w_vt = w[:, 2 * d:3 * d].T.astype(BF16)
            shift = _softmax_shift(c_q_gain[i], c_k_gain[i])
            gains = jnp.stack([_pair_gain(c_q_gain[i], scale * LOG2E), _pair_gain(c_k_gain[i])])
            bf = jnp.pad(b_forget[i].astype(F32), (0, LANES - n_heads_c)).reshape(1, LANES)
            eq, ek = _spare_lane_placement(n_heads_c, shift)
            q_aug, k_aug, vt, cfirst, clast = _in_odd(x2, g1, w_qkf, w_vt, eq, ek, gains, bf, seq, tm, n_heads_c)
            tq = min(FOX_TQ, seq)
            cfirst = cfirst.reshape(bsz, seq // tm, LANES)[:, ::tq // tm, :n_heads_c].transpose(0, 2, 1)
            clast = clast.reshape(bsz, seq // tm, LANES)[:, :, :n_heads_c].transpose(0, 2, 1)
            fox = functools.partial(_fox, tq=tq, sub=256)
            y = lax.cond(shift <= MAX_FIXED_SHIFT,
                         lambda *a: fox(*a, online_max=False),
                         lambda *a: fox(*a, online_max=True),
                         cfirst, clast, q_aug, k_aug, vt)
            x2 = _post([y], x2, w_out_odd[i].astype(BF16), g2, wu, wd, tm)
    return x2.reshape(bsz, seq, d)


def kernel(x, g_mix, g_mlp, w_in_even, a_q_gain, a_k_gain, b_q_gain, b_k_gain, b_sinks, w_out_even,
           w_in_odd, b_forget, c_q_gain, c_k_gain, w_out_odd, w_up, w_down):
    return _forward(x, g_mix, g_mlp, w_in_even, a_q_gain, a_k_gain, b_q_gain, b_k_gain, b_sinks,
                    w_out_even, w_in_odd, b_forget, c_q_gain, c_k_gain, w_out_odd, w_up, w_down)
```
